```python
import jax, jax.numpy as jnp
from jax import lax
import numpy as np

D_MODEL = 1024
BATCH = 8
SEQ = 4096
DEPTH = 1

CHUNK = 64
N_LEFT_CHUNKS = 8
BAND = (N_LEFT_CHUNKS + 1) * CHUNK
A_HEADS = 8
A_HEAD_DIM = 64
A_WIDTH = A_HEADS * A_HEAD_DIM
REL_CLIP = 256
B_HEADS = 8
B_HEAD_DIM = 64
B_WIDTH = B_HEADS * B_HEAD_DIM
IDX_HEADS = 8
IDX_DIM = 32
TOPK_MAX = 256
Q_BLOCK = 128
ROPE_THETA = 500000.0
ROT_DIM_B = B_HEAD_DIM // 4
ROT_DIM_IDX = IDX_DIM // 4
EPS = 1e-6
NEG = -1e30

SPLIT_SIZES = (
    A_WIDTH, A_WIDTH, A_WIDTH, A_WIDTH,
    B_WIDTH, B_HEAD_DIM, B_HEAD_DIM, B_WIDTH,
    IDX_HEADS * IDX_DIM, IDX_DIM, IDX_HEADS,
    D_MODEL, D_MODEL,
)
IN_WIDTH = sum(SPLIT_SIZES)

kernel_name = "hybrid_chunked_relpos_dsa_gated_block"


def rms_norm(x, g):
    xf = x.astype(jnp.float32)
    y = xf * lax.rsqrt(jnp.mean(xf * xf, axis=-1, keepdims=True) + EPS)
    return (y * g.astype(jnp.float32)).astype(x.dtype)


def partial_rope(x, rot_dim):
    S = x.shape[1]
    half = rot_dim // 2
    inv = ROPE_THETA ** (-jnp.arange(half, dtype=jnp.float32) / half)
    ang = jnp.arange(S, dtype=jnp.float32)[:, None] * inv[None, :]
    cos = jnp.cos(ang)[None, :, None, :].astype(x.dtype)
    sin = jnp.sin(ang)[None, :, None, :].astype(x.dtype)
    x1, x2, xp = x[..., :half], x[..., half:rot_dim], x[..., rot_dim:]
    return jnp.concatenate([x1 * cos - x2 * sin, x2 * cos + x1 * sin, xp], axis=-1)


def chunked_relpos_attention(q, k, v, rel_bias):
    B, S, H, d = q.shape
    n_chunks = S // CHUNK
    pad = N_LEFT_CHUNKS * CHUNK
    k_pad = jnp.pad(k, ((0, 0), (pad, 0), (0, 0), (0, 0)))
    v_pad = jnp.pad(v, ((0, 0), (pad, 0), (0, 0), (0, 0)))
    i = jnp.arange(CHUNK)[:, None]
    j = jnp.arange(BAND)[None, :]
    dist = pad + i - j
    bias = rel_bias.astype(jnp.float32)[:, jnp.clip(dist, -REL_CLIP, REL_CLIP) + REL_CLIP]
    scale = d ** -0.5

    def one_chunk(c):
        start = c * CHUNK
        qc = lax.dynamic_slice_in_dim(q, start, CHUNK, axis=1)
        kc = lax.dynamic_slice_in_dim(k_pad, start, BAND, axis=1)
        vc = lax.dynamic_slice_in_dim(v_pad, start, BAND, axis=1)
        s = jnp.einsum('bqhd,bkhd->bhqk', qc, kc).astype(jnp.float32) * scale + bias[None]
        valid = (start - pad + j) >= 0
        s = jnp.where(valid[None, None], s, NEG)
        p = jax.nn.softmax(s, axis=-1).astype(v.dtype)
        return jnp.einsum('bhqk,bkhd->bqhd', p, vc)

    out = lax.map(one_chunk, jnp.arange(n_chunks))
    return out.transpose(1, 0, 2, 3, 4).reshape(B, S, H * d)


def dsa_sparse_attention(q, k, v, iq, ik, iw):
    B, S, H, d = q.shape
    top_k = min(TOPK_MAX, S // 4)
    n_blocks = S // Q_BLOCK
    key_chunk = jnp.arange(S) // CHUNK
    scale = d ** -0.5
    gather = jax.vmap(lambda a, idx: a[idx])

    def one_block(blk):
        start = blk * Q_BLOCK
        qb = lax.dynamic_slice_in_dim(q, start, Q_BLOCK, axis=1)
        iqb = lax.dynamic_slice_in_dim(iq, start, Q_BLOCK, axis=1)
        iwb = lax.dynamic_slice_in_dim(iw, start, Q_BLOCK, axis=1)
        q_chunk = (start + jnp.arange(Q_BLOCK)) // CHUNK
        logits = jax.nn.relu(jnp.einsum('bqhd,bsd->bqhs', iqb, ik))
        score = jnp.einsum('bqhs,bqh->bqs', logits, iwb).astype(jnp.float32)
        adm = key_chunk[None, :] <= q_chunk[:, None]
        score = jnp.where(adm[None], score, -jnp.inf)
        _, idx = lax.top_k(score, top_k)
        sel_valid = (idx // CHUNK) <= q_chunk[None, :, None]
        ks = gather(k, idx)
        vs = gather(v, idx)
        s = jnp.einsum('bqhd,bqkd->bqhk', qb, ks).astype(jnp.float32) * scale
        s = jnp.where(sel_valid[:, :, None, :], s, NEG)
        p = jax.nn.softmax(s, axis=-1).astype(v.dtype)
        return jnp.einsum('bqhk,bqkd->bqhd', p, vs)

    out = lax.map(one_block, jnp.arange(n_blocks))
    return out.transpose(1, 0, 2, 3, 4).reshape(B, S, H * d)


def setup_inputs(seed: int = 0) -> dict:
    key = jax.random.key(seed)
    ks = jax.random.split(key, 10)
    nrm = jax.random.normal
    return {
        "x": nrm(ks[0], (BATCH, SEQ, D_MODEL), jnp.float32),
        "norm_gain": 1.0 + 0.05 * nrm(ks[1], (DEPTH, D_MODEL), jnp.float32),
        "w_in": nrm(ks[2], (DEPTH, D_MODEL, IN_WIDTH), jnp.float32) * D_MODEL ** -0.5,
        "b_merge": 0.05 * nrm(ks[3], (DEPTH, 2, D_MODEL), jnp.float32),
        "rel_bias": 0.2 * nrm(ks[4], (DEPTH, A_HEADS, 2 * REL_CLIP + 1), jnp.float32),
        "w_branch_a": nrm(ks[5], (DEPTH, A_WIDTH, D_MODEL), jnp.float32) * A_WIDTH ** -0.5,
        "w_branch_b": nrm(ks[6], (DEPTH, B_WIDTH, D_MODEL), jnp.float32) * B_WIDTH ** -0.5,
        "w_out": nrm(ks[7], (DEPTH, D_MODEL, D_MODEL), jnp.float32) * D_MODEL ** -0.5,
        "final_norm_gain": 1.0 + 0.05 * nrm(ks[8], (D_MODEL,), jnp.float32),
    }


def reference(x, norm_gain, w_in, b_merge, rel_bias, w_branch_a, w_branch_b, w_out, final_norm_gain):
    B, S, _ = x.shape
    offsets = [int(o) for o in np.cumsum(SPLIT_SIZES)[:-1]]
    h = x
    for layer in range(DEPTH):
        xn = rms_norm(h, norm_gain[layer])
        proj = xn @ w_in[layer]
        qa, ka, va, ga, qb, kb, vb, gb, iq, ik, iw, za, zb = jnp.split(proj, offsets, axis=-1)
        qa = qa.reshape(B, S, A_HEADS, A_HEAD_DIM)
        ka = ka.reshape(B, S, A_HEADS, A_HEAD_DIM)
        va = va.reshape(B, S, A_HEADS, A_HEAD_DIM)
        ya = chunked_relpos_attention(qa, ka, va, rel_bias[layer]) * jax.nn.silu(ga)
        qb = partial_rope(qb.reshape(B, S, B_HEADS, B_HEAD_DIM), ROT_DIM_B)
        kb = partial_rope(kb[:, :, None, :], ROT_DIM_B)[:, :, 0]
        iq = partial_rope(iq.reshape(B, S, IDX_HEADS, IDX_DIM), ROT_DIM_IDX)
        ik = partial_rope(ik[:, :, None, :], ROT_DIM_IDX)[:, :, 0]
        iw = iw * (IDX_HEADS ** -0.5 * IDX_DIM ** -0.5)
        yb = dsa_sparse_attention(qb, kb, vb, iq, ik, iw) * jax.nn.silu(gb)
        gate_a = jax.nn.sigmoid(za + b_merge[layer, 0])
        gate_b = jax.nn.sigmoid(zb + b_merge[layer, 1])
        merged = gate_a * (ya @ w_branch_a[layer]) + gate_b * (yb @ w_branch_b[layer])
        h = h + merged @ w_out[layer]
    return rms_norm(h, final_norm_gain)
```

```python
import functools

import jax
import jax.numpy as jnp
import numpy as np
from jax import lax
from jax.experimental import pallas as pl
from jax.experimental.pallas import tpu as pltpu

D_MODEL = 1024
CHUNK = 64
N_LEFT_CHUNKS = 8
A_HEADS = 8
A_HEAD_DIM = 64
A_WIDTH = A_HEADS * A_HEAD_DIM
REL_CLIP = 256
B_HEADS = 8
B_HEAD_DIM = 64
B_WIDTH = B_HEADS * B_HEAD_DIM
IDX_HEADS = 8
IDX_DIM = 32
TOPK_MAX = 256
ROPE_THETA = 500000.0
ROT_DIM_B = B_HEAD_DIM // 4
ROT_DIM_IDX = IDX_DIM // 4
EPS = 1e-6
NEG = -1e30

SPLIT_SIZES = (
    A_WIDTH, A_WIDTH, A_WIDTH, A_WIDTH,
    B_WIDTH, B_HEAD_DIM, B_HEAD_DIM, B_WIDTH,
    IDX_HEADS * IDX_DIM, IDX_DIM, IDX_HEADS,
    D_MODEL, D_MODEL,
)

LANES = 128
QB = 128
A_BAND = QB + N_LEFT_CHUNKS * CHUNK
A_VARIANTS = N_LEFT_CHUNKS * CHUNK // QB + 1
PROJ_TM = 512
OUT_TM = 256
VMEM_LIMIT = 48 * 1024 * 1024

BF16 = jnp.bfloat16
F32 = jnp.float32


def _rms(x, g):
    ms = jnp.mean(x * x, axis=-1, keepdims=True)
    return (x * lax.rsqrt(ms + EPS)) * g


def _lane_iota(shape):
    return lax.broadcasted_iota(jnp.int32, shape, len(shape) - 1)


def _rope(xg, c, s, half):
    lane = _lane_iota(xg.shape)
    first = (lane % (2 * half)) < half
    partner = jnp.where(first, pltpu.roll(xg, LANES - half, 1), pltpu.roll(xg, half, 1))
    return xg * c + partner * s


def _proj_kernel(x_ref, g_ref, w_ref, tab_ref,
                 qa_ref, ka_ref, va_ref, qb_ref, iq_ref, kb_ref, vb_ref, ik_ref, iw_ref):
    xn = _rms(x_ref[...], g_ref[...])
    p = jnp.dot(xn.astype(BF16), w_ref[...], preferred_element_type=F32)
    tm = p.shape[0]
    nblk = tm // QB
    lane = _lane_iota((tm, LANES))

    a_scale = A_HEAD_DIM ** -0.5
    b_scale = B_HEAD_DIM ** -0.5
    qa_ref[...] = (p[:, 0:A_WIDTH] * a_scale).astype(BF16)
    ka_ref[...] = p[:, A_WIDTH:2 * A_WIDTH].astype(BF16)
    va_ref[...] = p[:, 2 * A_WIDTH:3 * A_WIDTH].astype(BF16)

    cq = tab_ref[:, 0:LANES]
    sq = tab_ref[:, LANES:2 * LANES]
    ci = tab_ref[:, 2 * LANES:3 * LANES]
    si = tab_ref[:, 3 * LANES:4 * LANES]

    base = 3 * A_WIDTH
    for g in range(B_WIDTH // LANES):
        xg = _rope(p[:, base + g * LANES: base + (g + 1) * LANES], cq, sq, ROT_DIM_B // 2) * b_scale
        per_group = LANES // B_HEAD_DIM
        for r in range(per_group):
            h = g * per_group + r
            piece = xg if r == 0 else pltpu.roll(xg, LANES - r * B_HEAD_DIM, 1)
            piece = jnp.where(lane < B_HEAD_DIM, piece, 0.0).astype(BF16)
            for j in range(nblk):
                qb_ref[j, h] = piece[j * QB:(j + 1) * QB]

    base = 3 * A_WIDTH + B_WIDTH
    for g in range(IDX_HEADS * IDX_DIM // LANES):
        xg = _rope(p[:, base + g * LANES: base + (g + 1) * LANES], ci, si, ROT_DIM_IDX // 2)
        per_group = LANES // IDX_DIM
        for r in range(per_group):
            h = g * per_group + r
            piece = xg if r == 0 else pltpu.roll(xg, LANES - r * IDX_DIM, 1)
            piece = jnp.where(lane < IDX_DIM, piece, 0.0).astype(BF16)
            for j in range(nblk):
                iq_ref[j, h] = piece[j * QB:(j + 1) * QB]

    base = 3 * A_WIDTH + B_WIDTH + IDX_HEADS * IDX_DIM
    is_k = lane < B_HEAD_DIM
    kv = _rope(p[:, base:base + LANES], jnp.where(is_k, cq, 1.0), jnp.where(is_k, sq, 0.0),
               ROT_DIM_B // 2)
    kb_ref[...] = jnp.where(is_k, kv, 0.0).astype(BF16)
    vb_ref[...] = jnp.where(is_k, pltpu.roll(kv, LANES - B_HEAD_DIM, 1), 1.0).astype(BF16)

    base = base + LANES
    is_ik = lane < IDX_DIM
    kw = _rope(p[:, base:base + LANES], jnp.where(is_ik, ci, 1.0), jnp.where(is_ik, si, 0.0),
               ROT_DIM_IDX // 2)
    ik_ref[...] = jnp.where(is_ik, kw, 0.0).astype(BF16)
    iw_scale = IDX_HEADS ** -0.5 * IDX_DIM ** -0.5
    iw_ref[...] = pltpu.roll(kw, LANES - IDX_DIM, 1) * iw_scale


def _proj_call(x2, gain, w1, tab, batch, seq):
    m = x2.shape[0]
    tm = PROJ_TM
    n_s = seq // tm
    nblk = tm // QB
    row = lambda s, b: (b * n_s + s, 0)
    blk4 = lambda s, b: (b * n_s + s, 0, 0, 0)
    wide = lambda width, dtype: jax.ShapeDtypeStruct((m, width), dtype)
    stacked = jax.ShapeDtypeStruct((m // QB, B_HEADS, QB, LANES), BF16)
    return pl.pallas_call(
        _proj_kernel,
        grid=(n_s, batch),
        in_specs=[
            pl.BlockSpec((tm, D_MODEL), row),
            pl.BlockSpec((1, D_MODEL), lambda s, b: (0, 0)),
            pl.BlockSpec(w1.shape, lambda s, b: (0, 0)),
            pl.BlockSpec((tm, 4 * LANES), lambda s, b: (s, 0)),
        ],
        out_specs=[
            pl.BlockSpec((tm, A_WIDTH), row),
            pl.BlockSpec((tm, A_WIDTH), row),
            pl.BlockSpec((tm, A_WIDTH), row),
            pl.BlockSpec((nblk, B_HEADS, QB, LANES), blk4),
            pl.BlockSpec((nblk, IDX_HEADS, QB, LANES), blk4),
            pl.BlockSpec((tm, LANES), row),
            pl.BlockSpec((tm, LANES), row),
            pl.BlockSpec((tm, LANES), row),
            pl.BlockSpec((tm, LANES), row),
        ],
        out_shape=[
            wide(A_WIDTH, BF16), wide(A_WIDTH, BF16), wide(A_WIDTH, BF16),
            stacked, stacked,
            wide(LANES, BF16), wide(LANES, BF16), wide(LANES, BF16), wide(LANES, F32),
        ],
        compiler_params=pltpu.CompilerParams(
            dimension_semantics=("arbitrary", "arbitrary"), vmem_limit_bytes=VMEM_LIMIT),
        name="proj",
    )(x2, gain, w1, tab)


def _attn_a_kernel(q_ref, k_ref, v_ref, bias_ref, o_ref):
    qi = pl.program_id(1)
    start = pl.multiple_of(jnp.maximum(qi * QB - N_LEFT_CHUNKS * CHUNK, 0), QB)
    q = q_ref[0]
    for h in range(A_HEADS):
        cols = slice(h * A_HEAD_DIM, (h + 1) * A_HEAD_DIM)
        kh = k_ref[0, pl.ds(start, A_BAND), cols]
        vh = v_ref[0, pl.ds(start, A_BAND), cols]
        s = lax.dot_general(q[:, cols], kh, (((1,), (1,)), ((), ())),
                            preferred_element_type=F32) + bias_ref[0, h]
        m = jnp.max(s, axis=-1, keepdims=True)
        e = jnp.exp(s - m)
        l = jnp.sum(e, axis=-1, keepdims=True)
        o = jnp.dot(e.astype(BF16), vh, preferred_element_type=F32)
        o_ref[0, :, cols] = o / l


def _attn_a_call(qa, ka, va, bias):
    batch, seq, _ = qa.shape
    return pl.pallas_call(
        _attn_a_kernel,
        grid=(batch, seq // QB),
        in_specs=[
            pl.BlockSpec((1, QB, A_WIDTH), lambda b, i: (b, i, 0)),
            pl.BlockSpec((1, seq, A_WIDTH), lambda b, i: (b, 0, 0)),
            pl.BlockSpec((1, seq, A_WIDTH), lambda b, i: (b, 0, 0)),
            pl.BlockSpec((1, A_HEADS, QB, A_BAND),
                         lambda b, i: (jnp.minimum(i, A_VARIANTS - 1), 0, 0, 0)),
        ],
        out_specs=pl.BlockSpec((1, QB, A_WIDTH), lambda b, i: (b, i, 0)),
        out_shape=jax.ShapeDtypeStruct((batch, seq, A_WIDTH), F32),
        compiler_params=pltpu.CompilerParams(
            dimension_semantics=("arbitrary", "arbitrary"), vmem_limit_bytes=VMEM_LIMIT),
        name="attn_a",
    )(qa, ka, va, bias)


def _rel_bias_tiles(rel_bias):
    cpb = QB // CHUNK
    r = np.arange(QB)
    c = np.arange(A_BAND)
    idx = np.zeros((A_VARIANTS, QB, A_BAND), np.int32)
    ok = np.zeros((A_VARIANTS, QB, A_BAND), bool)
    for v in range(A_VARIANTS):
        c0 = v * cpb
        band_start = max(c0 - N_LEFT_CHUNKS, 0)
        q_chunk = c0 + r[:, None] // CHUNK
        k_chunk = band_start + c[None, :] // CHUNK
        delta = q_chunk - k_chunk
        dist = delta * CHUNK + (r[:, None] % CHUNK) - (c[None, :] % CHUNK)
        idx[v] = np.clip(dist, -REL_CLIP, REL_CLIP) + REL_CLIP
        ok[v] = (delta >= 0) & (delta <= N_LEFT_CHUNKS)
    tiles = rel_bias.astype(F32)[:, idx]
    tiles = jnp.where(ok[None], tiles, NEG)
    return tiles.transpose(1, 0, 2, 3)


def _ordered_to_float(u):
    bits = jnp.where(u < 0, u ^ jnp.int32(-2147483648), ~u)
    return pltpu.bitcast(bits, F32)


def _attn_b_kernel(q_ref, iq_ref, iw_ref, kb_ref, vb_ref, ik_ref, o_ref,
                   sc_ref, iwb_ref, thr_ref, cut_ref):
    qi = pl.program_id(1)
    n_tiles = qi + 1
    rows = B_HEADS * QB
    dn = (((1,), (1,)), ((), ()))

    qs = q_ref[0].reshape(rows, LANES)
    iqs = iq_ref[0].reshape(rows, LANES)
    iw = iw_ref[0]
    for h in range(IDX_HEADS):
        iwb_ref[h] = jnp.broadcast_to(iw[:, h:h + 1], (QB, LANES))

    def score_body(j, carry):
        ikt = ik_ref[0, pl.ds(pl.multiple_of(j * QB, QB), QB), :]
        lg = lax.dot_general(iqs, ikt, dn, preferred_element_type=F32)
        lg = jnp.maximum(lg, 0.0).reshape(IDX_HEADS, QB, LANES) * iwb_ref[...]
        sc_ref[j] = jnp.sum(lg, axis=0) + 0.0
        return carry
    lax.fori_loop(0, n_tiles, score_body, 0)

    row = lax.broadcasted_iota(jnp.int32, (QB, LANES), 0)
    lane = _lane_iota((QB, LANES))
    sc_ref[qi] = jnp.where((row < CHUNK) & (lane >= CHUNK), -jnp.inf, sc_ref[qi])

    thr_ref[...] = jnp.full((QB, LANES), -jnp.inf, F32)
    cut_ref[...] = jnp.full((QB, LANES), -1, jnp.int32)

    def count(pred):
        def body(j, acc):
            return acc + jnp.where(pred(sc_ref[j], j), 1.0, 0.0)
        acc = lax.fori_loop(0, n_tiles, body, jnp.zeros((QB, LANES), F32))
        return jnp.broadcast_to(jnp.sum(acc, axis=1, keepdims=True), (QB, LANES))

    k_sel = float(TOPK_MAX)

    @pl.when(n_tiles * QB - CHUNK > TOPK_MAX)
    def _search():
        def bit_body(i, u):
            trial = u | jnp.left_shift(jnp.int32(1), 31 - i)
            cand = _ordered_to_float(trial)
            cnt = count(lambda s, j: s >= cand)
            return jnp.where(cnt >= k_sel, trial, u)
        u = lax.fori_loop(0, 32, bit_body, jnp.zeros((QB, LANES), jnp.int32))
        thr = _ordered_to_float(u)
        thr_ref[...] = thr
        n_gt = count(lambda s, j: s > thr)
        n_ge = count(lambda s, j: s >= thr)
        need = k_sel - n_gt
        cut_ref[...] = jnp.full((QB, LANES), 2 ** 30, jnp.int32)

        @pl.when(jnp.max(n_ge) > k_sel)
        def _ties():
            def idx_body(i, x):
                trial = x | jnp.left_shift(jnp.int32(1), 11 - i)
                cnt = count(lambda s, j: (s == thr) & (lane + j * QB < trial))
                return jnp.where(cnt < need, trial, x)
            cut_ref[...] = lax.fori_loop(0, 12, idx_body, jnp.zeros((QB, LANES), jnp.int32))

    thr = thr_ref[...]
    cut = cut_ref[...]

    def att_body(j, carry):
        m_old, acc = carry
        off = pl.multiple_of(j * QB, QB)
        kt = kb_ref[0, pl.ds(off, QB), :]
        vt = vb_ref[0, pl.ds(off, QB), :]
        sc = sc_ref[j]
        sel = (sc > thr) | ((sc == thr) & (lane + j * QB <= cut))
        s = lax.dot_general(qs, kt, dn, preferred_element_type=F32)
        s = jnp.where(sel[None], s.reshape(B_HEADS, QB, LANES), NEG).reshape(rows, LANES)
        m_new = jnp.maximum(m_old, jnp.max(s, axis=-1, keepdims=True))
        p = jnp.exp(s - m_new)
        acc = acc * jnp.exp(m_old - m_new) + jnp.dot(p.astype(BF16), vt, preferred_element_type=F32)
        return m_new, acc
    m0 = jnp.full((rows, 1), NEG, F32)
    acc0 = jnp.zeros((rows, LANES), F32)
    _, acc = lax.fori_loop(0, n_tiles, att_body, (m0, acc0))

    out = acc / pltpu.roll(acc, B_HEAD_DIM, 1)
    for g in range(B_WIDTH // LANES):
        lo = out[(2 * g) * QB:(2 * g + 1) * QB]
        hi = pltpu.roll(out[(2 * g + 1) * QB:(2 * g + 2) * QB], B_HEAD_DIM, 1)
        o_ref[0, :, g * LANES:(g + 1) * LANES] = jnp.where(lane < B_HEAD_DIM, lo, hi)


def _attn_b_call(qb, iq, iw, kb, vb, ik, batch, seq):
    n_q = seq // QB
    blk = lambda b, i: (b * n_q + i, 0, 0, 0)
    res = lambda b, i: (b, 0, 0)
    return pl.pallas_call(
        _attn_b_kernel,
        grid=(batch, n_q),
        in_specs=[
            pl.BlockSpec((1, B_HEADS, QB, LANES), blk),
            pl.BlockSpec((1, IDX_HEADS, QB, LANES), blk),
            pl.BlockSpec((1, QB, LANES), lambda b, i: (b, i, 0)),
            pl.BlockSpec((1, seq, LANES), res),
            pl.BlockSpec((1, seq, LANES), res),
            pl.BlockSpec((1, seq, LANES), res),
        ],
        out_specs=pl.BlockSpec((1, QB, B_WIDTH), lambda b, i: (b, i, 0)),
        out_shape=jax.ShapeDtypeStruct((batch, seq, B_WIDTH), F32),
        scratch_shapes=[
            pltpu.VMEM((n_q, QB, LANES), F32),
            pltpu.VMEM((IDX_HEADS, QB, LANES), F32),
            pltpu.VMEM((QB, LANES), F32),
            pltpu.VMEM((QB, LANES), jnp.int32),
        ],
        compiler_params=pltpu.CompilerParams(
            dimension_semantics=("arbitrary", "arbitrary"), vmem_limit_bytes=VMEM_LIMIT),
        name="attn_b",
    )(qb, iq, iw, kb, vb, ik)


def _sigmoid(z):
    return 1.0 / (1.0 + jnp.exp(-z))


def _out_kernel(x_ref, g_ref, wg_ref, bm_ref, ya_ref, yb_ref, wa_ref, wb_ref, wo_ref, fg_ref, o_ref):
    x = x_ref[...]
    xn = _rms(x, g_ref[...]).astype(BF16)

    def branch(y_ref, col, w_ref, bias_row):
        gate = jnp.dot(xn, wg_ref[:, col:col + A_WIDTH], preferred_element_type=F32)
        y = (y_ref[...] * (gate * _sigmoid(gate))).astype(BF16)
        pr = jnp.dot(y, w_ref[...], preferred_element_type=F32)
        zcol = 2 * A_WIDTH + bias_row * D_MODEL
        z = jnp.dot(xn, wg_ref[:, zcol:zcol + D_MODEL], preferred_element_type=F32)
        return _sigmoid(z + bm_ref[bias_row:bias_row + 1, :]) * pr

    merged = branch(ya_ref, 0, wa_ref, 0) + branch(yb_ref, A_WIDTH, wb_ref, 1)
    h = x + jnp.dot(merged.astype(BF16), wo_ref[...], preferred_element_type=F32)
    o_ref[...] = _rms(h, fg_ref[...])


def _out_call(x2, gain, w2, bm, ya, yb, wa, wb, wo, fgain):
    m = x2.shape[0]
    tm = OUT_TM
    row = lambda i: (i, 0)
    full = lambda a: pl.BlockSpec(a.shape, lambda i: (0, 0))
    return pl.pallas_call(
        _out_kernel,
        grid=(m // tm,),
        in_specs=[
            pl.BlockSpec((tm, D_MODEL), row), full(gain), full(w2), full(bm),
            pl.BlockSpec((tm, A_WIDTH), row), pl.BlockSpec((tm, B_WIDTH), row),
            full(wa), full(wb), full(wo), full(fgain),
        ],
        out_specs=pl.BlockSpec((tm, D_MODEL), row),
        out_shape=jax.ShapeDtypeStruct((m, D_MODEL), F32),
        compiler_params=pltpu.CompilerParams(
            dimension_semantics=("arbitrary",), vmem_limit_bytes=VMEM_LIMIT),
        name="out",
    )(x2, gain, w2, bm, ya, yb, wa, wb, wo, fgain)


def _rope_tables(seq):
    pos = jnp.arange(seq, dtype=F32)[:, None]

    def pattern(head_dim, rot_dim):
        half = rot_dim // 2
        inv = ROPE_THETA ** (-jnp.arange(half, dtype=F32) / half)
        ang = pos * inv[None, :]
        cos, sin = jnp.cos(ang), jnp.sin(ang)
        ones = jnp.ones((seq, head_dim - rot_dim), F32)
        c = jnp.concatenate([cos, cos, ones], axis=1)
        s = jnp.concatenate([-sin, sin, 0.0 * ones], axis=1)
        reps = LANES // head_dim
        return jnp.tile(c, (1, reps)), jnp.tile(s, (1, reps))

    cq, sq = pattern(B_HEAD_DIM, ROT_DIM_B)
    ci, si = pattern(IDX_DIM, ROT_DIM_IDX)
    return jnp.concatenate([cq, sq, ci, si], axis=1)


def kernel(x, norm_gain, w_in, b_merge, rel_bias, w_branch_a, w_branch_b, w_out, final_norm_gain):
    batch, seq, d = x.shape
    assert d == D_MODEL and seq % PROJ_TM == 0 and norm_gain.shape[0] == 1
    m = batch * seq
    x2 = x.reshape(m, d)
    offsets = [int(o) for o in np.cumsum(SPLIT_SIZES)[:-1]]
    (w_qa, w_ka, w_va, w_ga, w_qb, w_kb, w_vb, w_gb,
     w_iq, w_ik, w_iw, w_za, w_zb) = jnp.split(w_in[0], offsets, axis=1)
    pad = jnp.zeros((d, LANES - IDX_DIM - IDX_HEADS), w_in.dtype)
    w1 = jnp.concatenate([w_qa, w_ka, w_va, w_qb, w_iq, w_kb, w_vb, w_ik, w_iw, pad], axis=1).astype(BF16)
    w2 = jnp.concatenate([w_ga, w_gb, w_za, w_zb], axis=1).astype(BF16)

    qa, ka, va, qb, iq, kb, vb, ik, iw = _proj_call(x2, norm_gain, w1, _rope_tables(seq), batch, seq)

    to3 = lambda a: a.reshape(batch, seq, a.shape[-1])
    ya = _attn_a_call(to3(qa), to3(ka), to3(va), _rel_bias_tiles(rel_bias[0]))
    yb = _attn_b_call(qb, iq, to3(iw), to3(kb), to3(vb), to3(ik), batch, seq)

    out = _out_call(x2, norm_gain, w2, b_merge[0], ya.reshape(m, A_WIDTH), yb.reshape(m, B_WIDTH),
                    w_branch_a[0].astype(BF16), w_branch_b[0].astype(BF16), w_out[0].astype(BF16),
                    final_norm_gain.reshape(1, d))
    return out.reshape(batch, seq, d)
```

```python
import jax
import jax.numpy as jnp
import numpy as np
from jax import lax
from jax.experimental import pallas as pl
from jax.experimental.pallas import tpu as pltpu

D_MODEL = 1024
CHUNK = 64
N_LEFT_CHUNKS = 8
A_HEADS = 8
A_HEAD_DIM = 64
A_WIDTH = A_HEADS * A_HEAD_DIM
REL_CLIP = 256
B_HEADS = 8
B_HEAD_DIM = 64
B_WIDTH = B_HEADS * B_HEAD_DIM
IDX_HEADS = 8
IDX_DIM = 32
TOPK_MAX = 256
ROPE_THETA = 500000.0
ROT_DIM_B = B_HEAD_DIM // 4
ROT_DIM_IDX = IDX_DIM // 4
EPS = 1e-6
NEG = -1e30

SPLIT_SIZES = (
    A_WIDTH, A_WIDTH, A_WIDTH, A_WIDTH,
    B_WIDTH, B_HEAD_DIM, B_HEAD_DIM, B_WIDTH,
    IDX_HEADS * IDX_DIM, IDX_DIM, IDX_HEADS,
    D_MODEL, D_MODEL,
)

LANES = 128
SUBLANES = 8
QB = 128
KT = 2 * QB
A_BAND = QB + N_LEFT_CHUNKS * CHUNK
A_VARIANTS = N_LEFT_CHUNKS * CHUNK // QB + 1
PROJ_TM = 512
OUT_TM = 256
VMEM_LIMIT = 48 * 1024 * 1024

BF16 = jnp.bfloat16
F32 = jnp.float32
NT_DIMS = (((1,), (1,)), ((), ()))


def _rms(x, g):
    ms = jnp.mean(x * x, axis=-1, keepdims=True)
    return (x * lax.rsqrt(ms + EPS)) * g


def _lane_iota(shape):
    return lax.broadcasted_iota(jnp.int32, shape, len(shape) - 1)


def _rope(xg, c, s, half):
    lane = _lane_iota(xg.shape)
    first = (lane % (2 * half)) < half
    partner = jnp.where(first, pltpu.roll(xg, LANES - half, 1), pltpu.roll(xg, half, 1))
    return xg * c + partner * s


def _proj_kernel(x_ref, g_ref, w_ref, tab_ref,
                 qa_ref, ka_ref, va_ref, qb_ref, iq_ref, kb_ref, vb_ref, ik_ref, iw_ref):
    xn = _rms(x_ref[...], g_ref[...])
    p = jnp.dot(xn.astype(BF16), w_ref[...], preferred_element_type=F32)
    tm = p.shape[0]
    nblk = tm // QB
    lane = _lane_iota((tm, LANES))

    a_scale = A_HEAD_DIM ** -0.5
    b_scale = B_HEAD_DIM ** -0.5
    qa_ref[...] = (p[:, 0:A_WIDTH] * a_scale).astype(BF16)
    ka_ref[...] = p[:, A_WIDTH:2 * A_WIDTH].astype(BF16)
    va_ref[...] = p[:, 2 * A_WIDTH:3 * A_WIDTH].astype(BF16)

    cq = tab_ref[:, 0:LANES]
    sq = tab_ref[:, LANES:2 * LANES]
    ci = tab_ref[:, 2 * LANES:3 * LANES]
    si = tab_ref[:, 3 * LANES:4 * LANES]

    base = 3 * A_WIDTH
    for g in range(B_WIDTH // LANES):
        xg = _rope(p[:, base + g * LANES: base + (g + 1) * LANES], cq, sq, ROT_DIM_B // 2) * b_scale
        per_group = LANES // B_HEAD_DIM
        for r in range(per_group):
            h = g * per_group + r
            piece = xg if r == 0 else pltpu.roll(xg, LANES - r * B_HEAD_DIM, 1)
            piece = jnp.where(lane < B_HEAD_DIM, piece, 0.0).astype(BF16)
            for j in range(nblk):
                qb_ref[j, h] = piece[j * QB:(j + 1) * QB]

    base = 3 * A_WIDTH + B_WIDTH
    for g in range(IDX_HEADS * IDX_DIM // LANES):
        xg = _rope(p[:, base + g * LANES: base + (g + 1) * LANES], ci, si, ROT_DIM_IDX // 2)
        per_group = LANES // IDX_DIM
        for r in range(per_group):
            h = g * per_group + r
            piece = xg if r == 0 else pltpu.roll(xg, LANES - r * IDX_DIM, 1)
            piece = jnp.where(lane < IDX_DIM, piece, 0.0).astype(BF16)
            for j in range(nblk):
                iq_ref[j, h] = piece[j * QB:(j + 1) * QB]

    base = 3 * A_WIDTH + B_WIDTH + IDX_HEADS * IDX_DIM
    is_k = lane < B_HEAD_DIM
    kv = _rope(p[:, base:base + LANES], jnp.where(is_k, cq, 1.0), jnp.where(is_k, sq, 0.0),
               ROT_DIM_B // 2)
    kb_ref[...] = jnp.where(is_k, kv, 0.0).astype(BF16)
    vb_ref[...] = jnp.where(is_k, pltpu.roll(kv, LANES - B_HEAD_DIM, 1), 1.0).astype(BF16)

    base = base + LANES
    is_ik = lane < IDX_DIM
    kw = _rope(p[:, base:base + LANES], jnp.where(is_ik, ci, 1.0), jnp.where(is_ik, si, 0.0),
               ROT_DIM_IDX // 2)
    ik_ref[...] = jnp.where(is_ik, kw, 0.0).astype(BF16)
    iw_scale = IDX_HEADS ** -0.5 * IDX_DIM ** -0.5
    iw_ref[...] = pltpu.roll(kw, LANES - IDX_DIM, 1) * iw_scale


def _proj_call(x2, gain, w1, tab, batch, seq):
    m = x2.shape[0]
    tm = PROJ_TM
    n_s = seq // tm
    nblk = tm // QB
    row = lambda s, b: (b * n_s + s, 0)
    blk4 = lambda s, b: (b * n_s + s, 0, 0, 0)
    wide = lambda width, dtype: jax.ShapeDtypeStruct((m, width), dtype)
    stacked = jax.ShapeDtypeStruct((m // QB, B_HEADS, QB, LANES), BF16)
    return pl.pallas_call(
        _proj_kernel,
        grid=(n_s, batch),
        in_specs=[
            pl.BlockSpec((tm, D_MODEL), row),
            pl.BlockSpec((1, D_MODEL), lambda s, b: (0, 0)),
            pl.BlockSpec(w1.shape, lambda s, b: (0, 0)),
            pl.BlockSpec((tm, 4 * LANES), lambda s, b: (s, 0)),
        ],
        out_specs=[
            pl.BlockSpec((tm, A_WIDTH), row),
            pl.BlockSpec((tm, A_WIDTH), row),
            pl.BlockSpec((tm, A_WIDTH), row),
            pl.BlockSpec((nblk, B_HEADS, QB, LANES), blk4),
            pl.BlockSpec((nblk, IDX_HEADS, QB, LANES), blk4),
            pl.BlockSpec((tm, LANES), row),
            pl.BlockSpec((tm, LANES), row),
            pl.BlockSpec((tm, LANES), row),
            pl.BlockSpec((tm, LANES), row),
        ],
        out_shape=[
            wide(A_WIDTH, BF16), wide(A_WIDTH, BF16), wide(A_WIDTH, BF16),
            stacked, stacked,
            wide(LANES, BF16), wide(LANES, BF16), wide(LANES, BF16), wide(LANES, F32),
        ],
        compiler_params=pltpu.CompilerParams(
            dimension_semantics=("arbitrary", "arbitrary"), vmem_limit_bytes=VMEM_LIMIT),
        name="proj",
    )(x2, gain, w1, tab)


def _attn_a_kernel(q_ref, k_ref, v_ref, bias_ref, o_ref):
    qi = pl.program_id(1)
    start = pl.multiple_of(jnp.maximum(qi * QB - N_LEFT_CHUNKS * CHUNK, 0), QB)
    q = q_ref[0]
    for h in range(A_HEADS):
        cols = slice(h * A_HEAD_DIM, (h + 1) * A_HEAD_DIM)
        kh = k_ref[0, pl.ds(start, A_BAND), cols]
        vh = v_ref[0, pl.ds(start, A_BAND), cols]
        s = lax.dot_general(q[:, cols], kh, NT_DIMS, preferred_element_type=F32) + bias_ref[0, h]
        m = jnp.max(s, axis=-1, keepdims=True)
        e = jnp.exp(s - m)
        l = jnp.sum(e, axis=-1, keepdims=True)
        o = jnp.dot(e.astype(BF16), vh, preferred_element_type=F32)
        o_ref[0, :, cols] = o / l


def _attn_a_call(qa, ka, va, bias):
    batch, seq, _ = qa.shape
    return pl.pallas_call(
        _attn_a_kernel,
        grid=(batch, seq // QB),
        in_specs=[
            pl.BlockSpec((1, QB, A_WIDTH), lambda b, i: (b, i, 0)),
            pl.BlockSpec((1, seq, A_WIDTH), lambda b, i: (b, 0, 0)),
            pl.BlockSpec((1, seq, A_WIDTH), lambda b, i: (b, 0, 0)),
            pl.BlockSpec((1, A_HEADS, QB, A_BAND),
                         lambda b, i: (jnp.minimum(i, A_VARIANTS - 1), 0, 0, 0)),
        ],
        out_specs=pl.BlockSpec((1, QB, A_WIDTH), lambda b, i: (b, i, 0)),
        out_shape=jax.ShapeDtypeStruct((batch, seq, A_WIDTH), F32),
        compiler_params=pltpu.CompilerParams(
            dimension_semantics=("arbitrary", "arbitrary"), vmem_limit_bytes=VMEM_LIMIT),
        name="attn_a",
    )(qa, ka, va, bias)


def _rel_bias_tiles(rel_bias):
    pad = N_LEFT_CHUNKS * CHUNK
    width = pad + A_BAND
    length = width + QB
    n_edge = length - 1 - (2 * REL_CLIP + 1)
    assert n_edge % 2 == 0
    rb = rel_bias.astype(F32)
    h = rb.shape[0]
    f = jnp.concatenate([jnp.broadcast_to(rb[:, -1:], (h, n_edge // 2)), rb[:, ::-1],
                         jnp.broadcast_to(rb[:, :1], (h, n_edge // 2 + 1))], axis=1)
    f = jnp.roll(f, -(QB - 1), axis=1)
    toep = jnp.tile(f, (1, QB))[:, :QB * (length - 1)].reshape(h, QB, length - 1)[:, :, :width]

    cpb = QB // CHUNK
    r = np.arange(QB)
    c = np.arange(A_BAND)
    tiles = []
    for v in range(A_VARIANTS):
        c0 = v * cpb
        band_start = max(c0 - N_LEFT_CHUNKS, 0)
        delta = (c0 + r[:, None] // CHUNK) - (band_start + c[None, :] // CHUNK)
        ok = (delta >= 0) & (delta <= N_LEFT_CHUNKS)
        col0 = pad - (c0 - band_start) * CHUNK
        tiles.append(jnp.where(ok[None], toep[:, :, col0:col0 + A_BAND], NEG))
    return jnp.stack(tiles, axis=0)


def _ordered_to_float(u):
    bits = jnp.where(u < 0, u ^ jnp.int32(-2147483648), ~u)
    return pltpu.bitcast(bits, F32)


def _attn_b_kernel(q_ref, iq_ref, iw_ref, kb_ref, vb_ref, ik_ref, o_ref,
                   st_ref, sel_ref, mb_ref, s_ref, mrun_ref, acc_ref):
    qi = pl.program_id(1)
    n_tiles = qi + 1
    n_steps = (n_tiles + 1) // 2
    rows = B_HEADS * QB

    qs = q_ref[0].reshape(rows, LANES)
    iqs = iq_ref[0].reshape(rows, LANES)
    iw_t = iw_ref[0].T

    key_iota = lax.broadcasted_iota(jnp.int32, (KT, LANES), 0)
    q_lane = _lane_iota((1, LANES))
    key_limit = qi * QB + CHUNK + jnp.where(q_lane >= CHUNK, CHUNK, 0)

    def score_body(t, carry):
        off = pl.multiple_of(t * KT, KT)
        lg = lax.dot_general(ik_ref[0, pl.ds(off, KT), :], iqs, NT_DIMS,
                             preferred_element_type=F32)
        sc = None
        for h in range(IDX_HEADS):
            term = jnp.maximum(lg[:, h * QB:(h + 1) * QB], 0.0) * iw_t[h:h + 1, :]
            sc = term if sc is None else sc + term
        st_ref[pl.ds(off, KT), :] = jnp.where(key_iota + off < key_limit, sc + 0.0, -jnp.inf)
        return carry
    lax.fori_loop(0, n_steps, score_body, 0)

    sel_ref[0:1, :] = jnp.full((1, LANES), -jnp.inf, F32)
    sel_ref[1:2, :] = jnp.full((1, LANES), -1.0, F32)

    def count(pred):
        def body(t, acc):
            off = pl.multiple_of(t * KT, KT)
            hit = jnp.where(pred(st_ref[pl.ds(off, KT), :], key_iota + off), 1.0, 0.0)
            return acc + hit[:QB] + hit[QB:]
        acc = lax.fori_loop(0, n_steps, body, jnp.zeros((QB, LANES), F32))
        return jnp.sum(acc, axis=0, keepdims=True)

    k_sel = float(TOPK_MAX)

    @pl.when(n_tiles * QB - CHUNK > TOPK_MAX)
    def _search():
        def bit_body(i, u):
            trial = u | jnp.left_shift(jnp.int32(1), 31 - i)
            cand = _ordered_to_float(trial)
            cnt = count(lambda s, k: s >= cand)
            return jnp.where(cnt >= k_sel, trial, u)
        u = lax.fori_loop(0, 32, bit_body, jnp.zeros((1, LANES), jnp.int32))
        thr = _ordered_to_float(u)
        sel_ref[0:1, :] = thr
        n_gt = count(lambda s, k: s > thr)
        n_ge = count(lambda s, k: s >= thr)
        need = k_sel - n_gt
        sel_ref[1:2, :] = jnp.full((1, LANES), 2.0 ** 30, F32)

        @pl.when(jnp.max(n_ge) > k_sel)
        def _ties():
            def idx_body(i, x):
                trial = x | jnp.left_shift(jnp.int32(1), 11 - i)
                cnt = count(lambda s, k: (s == thr) & (k < trial))
                return jnp.where(cnt < need, trial, x)
            x = lax.fori_loop(0, 12, idx_body, jnp.zeros((1, LANES), jnp.int32))
            sel_ref[1:2, :] = x.astype(F32)

    thr = sel_ref[0:1, :]
    cut = sel_ref[1:2, :].astype(jnp.int32)

    half_iota = key_iota[:QB]

    def mask_body(t, carry):
        for half in range(KT // QB):
            off = pl.multiple_of(t * KT + half * QB, QB)
            sc = st_ref[pl.ds(off, QB), :]
            tie = jnp.where(half_iota + off <= cut, 0.0, NEG)
            m_t = jnp.where(sc > thr, 0.0, jnp.where(sc == thr, tie, NEG))
            mb_ref[t, :, half * QB:(half + 1) * QB] = m_t.T
        return carry
    lax.fori_loop(0, n_steps, mask_body, 0)

    mrun_ref[...] = jnp.full((rows, LANES), NEG, F32)

    def pass1(t, carry):
        off = pl.multiple_of(t * KT, KT)
        s = lax.dot_general(qs, kb_ref[0, pl.ds(off, KT), :], NT_DIMS,
                            preferred_element_type=F32)
        s = (s.reshape(B_HEADS, QB, KT) + mb_ref[t][None]).reshape(rows, KT)
        s_ref[t] = s
        mrun_ref[...] = jnp.maximum(mrun_ref[...], jnp.maximum(s[:, :LANES], s[:, LANES:]))
        return carry
    lax.fori_loop(0, n_steps, pass1, 0)

    m_row = jnp.max(mrun_ref[...], axis=1, keepdims=True)
    mrun_ref[...] = jnp.broadcast_to(m_row, (rows, LANES))
    acc_ref[...] = jnp.zeros((rows, LANES), F32)

    def pass2(t, carry):
        off = pl.multiple_of(t * KT, KT)
        m_b = mrun_ref[...]
        s = s_ref[t]
        p = jnp.concatenate([jnp.exp(s[:, :LANES] - m_b), jnp.exp(s[:, LANES:] - m_b)], axis=1)
        acc_ref[...] += jnp.dot(p.astype(BF16), vb_ref[0, pl.ds(off, KT), :],
                                preferred_element_type=F32)
        return carry
    lax.fori_loop(0, n_steps, pass2, 0)

    acc = acc_ref[...]
    out = acc / pltpu.roll(acc, B_HEAD_DIM, 1)
    lane = _lane_iota((QB, LANES))
    for g in range(B_WIDTH // LANES):
        lo = out[(2 * g) * QB:(2 * g + 1) * QB]
        hi = pltpu.roll(out[(2 * g + 1) * QB:(2 * g + 2) * QB], B_HEAD_DIM, 1)
        o_ref[0, :, g * LANES:(g + 1) * LANES] = jnp.where(lane < B_HEAD_DIM, lo, hi)


def _attn_b_call(qb, iq, iw, kb, vb, ik, batch, seq):
    n_q = seq // QB
    rows = B_HEADS * QB
    blk = lambda b, i: (b * n_q + i, 0, 0, 0)
    res = lambda b, i: (b, 0, 0)
    return pl.pallas_call(
        _attn_b_kernel,
        grid=(batch, n_q),
        in_specs=[
            pl.BlockSpec((1, B_HEADS, QB, LANES), blk),
            pl.BlockSpec((1, IDX_HEADS, QB, LANES), blk),
            pl.BlockSpec((1, QB, LANES), lambda b, i: (b, i, 0)),
            pl.BlockSpec((1, seq, LANES), res),
            pl.BlockSpec((1, seq, LANES), res),
            pl.BlockSpec((1, seq, LANES), res),
        ],
        out_specs=pl.BlockSpec((1, QB, B_WIDTH), lambda b, i: (b, i, 0)),
        out_shape=jax.ShapeDtypeStruct((batch, seq, B_WIDTH), F32),
        scratch_shapes=[
            pltpu.VMEM((seq, LANES), F32),
            pltpu.VMEM((SUBLANES, LANES), F32),
            pltpu.VMEM((seq // KT, QB, KT), F32),
            pltpu.VMEM((seq // KT, rows, KT), F32),
            pltpu.VMEM((rows, LANES), F32),
            pltpu.VMEM((rows, LANES), F32),
        ],
        compiler_params=pltpu.CompilerParams(
            dimension_semantics=("arbitrary", "arbitrary"), vmem_limit_bytes=VMEM_LIMIT),
        name="attn_b",
    )(qb, iq, iw, kb, vb, ik)


def _sigmoid(z):
    return 1.0 / (1.0 + jnp.exp(-z))


def _out_kernel(x_ref, g_ref, wg_ref, bm_ref, ya_ref, yb_ref, wa_ref, wb_ref, wo_ref, fg_ref, o_ref):
    x = x_ref[...]
    xn = _rms(x, g_ref[...]).astype(BF16)

    def branch(y_ref, col, w_ref, bias_row):
        gate = jnp.dot(xn, wg_ref[:, col:col + A_WIDTH], preferred_element_type=F32)
        y = (y_ref[...] * (gate * _sigmoid(gate))).astype(BF16)
        pr = jnp.dot(y, w_ref[...], preferred_element_type=F32)
        zcol = 2 * A_WIDTH + bias_row * D_MODEL
        z = jnp.dot(xn, wg_ref[:, zcol:zcol + D_MODEL], preferred_element_type=F32)
        return _sigmoid(z + bm_ref[bias_row:bias_row + 1, :]) * pr

    merged = branch(ya_ref, 0, wa_ref, 0) + branch(yb_ref, A_WIDTH, wb_ref, 1)
    h = x + jnp.dot(merged.astype(BF16), wo_ref[...], preferred_element_type=F32)
    o_ref[...] = _rms(h, fg_ref[...])


def _out_call(x2, gain, w2, bm, ya, yb, wa, wb, wo, fgain):
    m = x2.shape[0]
    tm = OUT_TM
    row = lambda i: (i, 0)
    full = lambda a: pl.BlockSpec(a.shape, lambda i: (0, 0))
    return pl.pallas_call(
        _out_kernel,
        grid=(m // tm,),
        in_specs=[
            pl.BlockSpec((tm, D_MODEL), row), full(gain), full(w2), full(bm),
            pl.BlockSpec((tm, A_WIDTH), row), pl.BlockSpec((tm, B_WIDTH), row),
            full(wa), full(wb), full(wo), full(fgain),
        ],
        out_specs=pl.BlockSpec((tm, D_MODEL), row),
        out_shape=jax.ShapeDtypeStruct((m, D_MODEL), F32),
        compiler_params=pltpu.CompilerParams(
            dimension_semantics=("arbitrary",), vmem_limit_bytes=VMEM_LIMIT),
        name="out",
    )(x2, gain, w2, bm, ya, yb, wa, wb, wo, fgain)


def _rope_tables(seq):
    pos = jnp.arange(seq, dtype=F32)[:, None]

    def pattern(head_dim, rot_dim):
        half = rot_dim // 2
        inv = ROPE_THETA ** (-jnp.arange(half, dtype=F32) / half)
        ang = pos * inv[None, :]
        cos, sin = jnp.cos(ang), jnp.sin(ang)
        ones = jnp.ones((seq, head_dim - rot_dim), F32)
        c = jnp.concatenate([cos, cos, ones], axis=1)
        s = jnp.concatenate([-sin, sin, 0.0 * ones], axis=1)
        reps = LANES // head_dim
        return jnp.tile(c, (1, reps)), jnp.tile(s, (1, reps))

    cq, sq = pattern(B_HEAD_DIM, ROT_DIM_B)
    ci, si = pattern(IDX_DIM, ROT_DIM_IDX)
    return jnp.concatenate([cq, sq, ci, si], axis=1)


def kernel(x, norm_gain, w_in, b_merge, rel_bias, w_branch_a, w_branch_b, w_out, final_norm_gain):
    batch, seq, d = x.shape
    assert d == D_MODEL and seq % PROJ_TM == 0 and norm_gain.shape[0] == 1
    m = batch * seq
    x2 = x.reshape(m, d)
    offsets = [int(o) for o in np.cumsum(SPLIT_SIZES)[:-1]]
    (w_qa, w_ka, w_va, w_ga, w_qb, w_kb, w_vb, w_gb,
     w_iq, w_ik, w_iw, w_za, w_zb) = jnp.split(w_in[0], offsets, axis=1)
    pad = jnp.zeros((d, LANES - IDX_DIM - IDX_HEADS), w_in.dtype)
    w1 = jnp.concatenate([w_qa, w_ka, w_va, w_qb, w_iq, w_kb, w_vb, w_ik, w_iw, pad], axis=1).astype(BF16)
    w2 = jnp.concatenate([w_ga, w_gb, w_za, w_zb], axis=1).astype(BF16)

    qa, ka, va, qb, iq, kb, vb, ik, iw = _proj_call(x2, norm_gain, w1, _rope_tables(seq), batch, seq)

    to3 = lambda a: a.reshape(batch, seq, a.shape[-1])
    ya = _attn_a_call(to3(qa), to3(ka), to3(va), _rel_bias_tiles(rel_bias[0]))
    yb = _attn_b_call(qb, iq, to3(iw), to3(kb), to3(vb), to3(ik), batch, seq)

    out = _out_call(x2, norm_gain, w2, b_merge[0], ya.reshape(m, A_WIDTH), yb.reshape(m, B_WIDTH),
                    w_branch_a[0].astype(BF16), w_branch_b[0].astype(BF16), w_out[0].astype(BF16),
                    final_norm_gain.reshape(1, d))
    return out.reshape(batch, seq, d)
```

```python
import jax
import jax.numpy as jnp
import numpy as np
from jax import lax
from jax.experimental import pallas as pl
from jax.experimental.pallas import tpu as pltpu

D_MODEL = 1024
CHUNK = 64
N_LEFT_CHUNKS = 8
A_HEADS = 8
A_HEAD_DIM = 64
A_WIDTH = A_HEADS * A_HEAD_DIM
REL_CLIP = 256
B_HEADS = 8
B_HEAD_DIM = 64
B_WIDTH = B_HEADS * B_HEAD_DIM
IDX_HEADS = 8
IDX_DIM = 32
TOPK_MAX = 256
ROPE_THETA = 500000.0
ROT_DIM_B = B_HEAD_DIM // 4
ROT_DIM_IDX = IDX_DIM // 4
EPS = 1e-6
NEG = -1e30

SPLIT_SIZES = (
    A_WIDTH, A_WIDTH, A_WIDTH, A_WIDTH,
    B_WIDTH, B_HEAD_DIM, B_HEAD_DIM, B_WIDTH,
    IDX_HEADS * IDX_DIM, IDX_DIM, IDX_HEADS,
    D_MODEL, D_MODEL,
)

LANES = 128
SUBLANES = 8
QB = 128
KT = 2 * QB
A_BAND = QB + N_LEFT_CHUNKS * CHUNK
A_VARIANTS = N_LEFT_CHUNKS * CHUNK // QB + 1
PROJ_TM = 512
OUT_TM = 256
VMEM_LIMIT = 48 * 1024 * 1024

BF16 = jnp.bfloat16
F32 = jnp.float32
NT_DIMS = (((1,), (1,)), ((), ()))


def _rms(x, g):
    ms = jnp.mean(x * x, axis=-1, keepdims=True)
    return (x * lax.rsqrt(ms + EPS)) * g


def _lane_iota(shape):
    return lax.broadcasted_iota(jnp.int32, shape, len(shape) - 1)


def _rope(xg, c, s, half):
    lane = _lane_iota(xg.shape)
    first = (lane % (2 * half)) < half
    partner = jnp.where(first, pltpu.roll(xg, LANES - half, 1), pltpu.roll(xg, half, 1))
    return xg * c + partner * s


def _proj_kernel(x_ref, g_ref, w_ref, tab_ref,
                 qa_ref, ka_ref, va_ref, qb_ref, iq_ref, kb_ref, vb_ref, ik_ref, iw_ref):
    xn = _rms(x_ref[...], g_ref[...])
    p = jnp.dot(xn.astype(BF16), w_ref[...], preferred_element_type=F32)
    tm = p.shape[0]
    nblk = tm // QB
    lane = _lane_iota((tm, LANES))

    a_scale = A_HEAD_DIM ** -0.5
    b_scale = B_HEAD_DIM ** -0.5
    for g in range(A_WIDTH // LANES):
        xg = p[:, g * LANES:(g + 1) * LANES] * a_scale
        for r in range(LANES // A_HEAD_DIM):
            own = (lane // A_HEAD_DIM) == r
            piece = jnp.where(own, xg, 0.0).astype(BF16)
            for j in range(nblk):
                qa_ref[j, g * (LANES // A_HEAD_DIM) + r] = piece[j * QB:(j + 1) * QB]
    ka_ref[...] = p[:, A_WIDTH:2 * A_WIDTH].astype(BF16)
    va_ref[...] = p[:, 2 * A_WIDTH:3 * A_WIDTH].astype(BF16)

    cq = tab_ref[:, 0:LANES]
    sq = tab_ref[:, LANES:2 * LANES]
    ci = tab_ref[:, 2 * LANES:3 * LANES]
    si = tab_ref[:, 3 * LANES:4 * LANES]

    base = 3 * A_WIDTH
    for g in range(B_WIDTH // LANES):
        xg = _rope(p[:, base + g * LANES: base + (g + 1) * LANES], cq, sq, ROT_DIM_B // 2) * b_scale
        per_group = LANES // B_HEAD_DIM
        for r in range(per_group):
            h = g * per_group + r
            piece = xg if r == 0 else pltpu.roll(xg, LANES - r * B_HEAD_DIM, 1)
            piece = jnp.where(lane < B_HEAD_DIM, piece, 0.0).astype(BF16)
            for j in range(nblk):
                qb_ref[j, h] = piece[j * QB:(j + 1) * QB]

    base = 3 * A_WIDTH + B_WIDTH
    for g in range(IDX_HEADS * IDX_DIM // LANES):
        xg = _rope(p[:, base + g * LANES: base + (g + 1) * LANES], ci, si, ROT_DIM_IDX // 2)
        per_group = LANES // IDX_DIM
        for r in range(per_group):
            h = g * per_group + r
            piece = xg if r == 0 else pltpu.roll(xg, LANES - r * IDX_DIM, 1)
            piece = jnp.where(lane < IDX_DIM, piece, 0.0).astype(BF16)
            for j in range(nblk):
                iq_ref[j, h] = piece[j * QB:(j + 1) * QB]

    base = 3 * A_WIDTH + B_WIDTH + IDX_HEADS * IDX_DIM
    is_k = lane < B_HEAD_DIM
    kv = _rope(p[:, base:base + LANES], jnp.where(is_k, cq, 1.0), jnp.where(is_k, sq, 0.0),
               ROT_DIM_B // 2)
    kb_ref[...] = jnp.where(is_k, kv, 0.0).astype(BF16)
    vb_ref[...] = jnp.where(is_k, pltpu.roll(kv, LANES - B_HEAD_DIM, 1), 1.0).astype(BF16)

    base = base + LANES
    is_ik = lane < IDX_DIM
    kw = _rope(p[:, base:base + LANES], jnp.where(is_ik, ci, 1.0), jnp.where(is_ik, si, 0.0),
               ROT_DIM_IDX // 2)
    ik_ref[...] = jnp.where(is_ik, kw, 0.0).astype(BF16)
    iw_scale = IDX_HEADS ** -0.5 * IDX_DIM ** -0.5
    iw_ref[...] = pltpu.roll(kw, LANES - IDX_DIM, 1) * iw_scale


def _proj_call(x2, gain, w1, tab, batch, seq):
    m = x2.shape[0]
    tm = PROJ_TM
    n_s = seq // tm
    nblk = tm // QB
    row = lambda s, b: (b * n_s + s, 0)
    blk4 = lambda s, b: (b * n_s + s, 0, 0, 0)
    wide = lambda width, dtype: jax.ShapeDtypeStruct((m, width), dtype)
    stacked = jax.ShapeDtypeStruct((m // QB, B_HEADS, QB, LANES), BF16)
    return pl.pallas_call(
        _proj_kernel,
        grid=(n_s, batch),
        in_specs=[
            pl.BlockSpec((tm, D_MODEL), row),
            pl.BlockSpec((1, D_MODEL), lambda s, b: (0, 0)),
            pl.BlockSpec(w1.shape, lambda s, b: (0, 0)),
            pl.BlockSpec((tm, 4 * LANES), lambda s, b: (s, 0)),
        ],
        out_specs=[
            pl.BlockSpec((nblk, A_HEADS, QB, LANES), blk4),
            pl.BlockSpec((tm, A_WIDTH), row),
            pl.BlockSpec((tm, A_WIDTH), row),
            pl.BlockSpec((nblk, B_HEADS, QB, LANES), blk4),
            pl.BlockSpec((nblk, IDX_HEADS, QB, LANES), blk4),
            pl.BlockSpec((tm, LANES), row),
            pl.BlockSpec((tm, LANES), row),
            pl.BlockSpec((tm, LANES), row),
            pl.BlockSpec((tm, LANES), row),
        ],
        out_shape=[
            stacked, wide(A_WIDTH, BF16), wide(A_WIDTH, BF16),
            stacked, stacked,
            wide(LANES, BF16), wide(LANES, BF16), wide(LANES, BF16), wide(LANES, F32),
        ],
        compiler_params=pltpu.CompilerParams(
            dimension_semantics=("arbitrary", "arbitrary"), vmem_limit_bytes=VMEM_LIMIT),
        name="proj",
    )(x2, gain, w1, tab)


def _attn_a_kernel(q_ref, k_ref, v_ref, bias_ref, o_ref, s_ref, p_ref, rinv_ref):
    qi = pl.program_id(1)
    start = pl.multiple_of(jnp.maximum(qi * QB - N_LEFT_CHUNKS * CHUNK, 0), QB)
    lane = _lane_iota((QB, LANES))
    per_group = LANES // A_HEAD_DIM
    group_cols = lambda h: slice((h // per_group) * LANES, (h // per_group + 1) * LANES)

    for h in range(A_HEADS):
        kg = k_ref[0, pl.ds(start, A_BAND), group_cols(h)]
        s_ref[h] = lax.dot_general(q_ref[0, h], kg, NT_DIMS, preferred_element_type=F32) + bias_ref[0, h]
    for h in range(A_HEADS):
        s = s_ref[h]
        e = jnp.exp(s - jnp.max(s, axis=-1, keepdims=True))
        p_ref[h] = e.astype(BF16)
        rinv_ref[h] = jnp.broadcast_to(1.0 / jnp.sum(e, axis=-1, keepdims=True), (QB, LANES))
    for g in range(A_WIDTH // LANES):
        out = None
        for r in range(per_group):
            h = g * per_group + r
            vg = v_ref[0, pl.ds(start, A_BAND), group_cols(h)]
            o = jnp.dot(p_ref[h], vg, preferred_element_type=F32) * rinv_ref[h]
            out = o if out is None else jnp.where((lane // A_HEAD_DIM) == r, o, out)
        o_ref[0, :, g * LANES:(g + 1) * LANES] = out


def _attn_a_call(qa, ka, va, bias):
    batch, seq, _ = ka.shape
    n_q = seq // QB
    return pl.pallas_call(
        _attn_a_kernel,
        grid=(batch, n_q),
        in_specs=[
            pl.BlockSpec((1, A_HEADS, QB, LANES), lambda b, i: (b * n_q + i, 0, 0, 0)),
            pl.BlockSpec((1, seq, A_WIDTH), lambda b, i: (b, 0, 0)),
            pl.BlockSpec((1, seq, A_WIDTH), lambda b, i: (b, 0, 0)),
            pl.BlockSpec((1, A_HEADS, QB, A_BAND),
                         lambda b, i: (jnp.minimum(i, A_VARIANTS - 1), 0, 0, 0)),
        ],
        out_specs=pl.BlockSpec((1, QB, A_WIDTH), lambda b, i: (b, i, 0)),
        out_shape=jax.ShapeDtypeStruct((batch, seq, A_WIDTH), F32),
        scratch_shapes=[
            pltpu.VMEM((A_HEADS, QB, A_BAND), F32),
            pltpu.VMEM((A_HEADS, QB, A_BAND), BF16),
            pltpu.VMEM((A_HEADS, QB, LANES), F32),
        ],
        compiler_params=pltpu.CompilerParams(
            dimension_semantics=("arbitrary", "arbitrary"), vmem_limit_bytes=VMEM_LIMIT),
        name="attn_a",
    )(qa, ka, va, bias)


def _rel_bias_tiles(rel_bias):
    pad = N_LEFT_CHUNKS * CHUNK
    width = pad + A_BAND
    length = width + QB
    n_edge = length - 1 - (2 * REL_CLIP + 1)
    assert n_edge % 2 == 0
    rb = rel_bias.astype(F32)
    h = rb.shape[0]
    f = jnp.concatenate([jnp.broadcast_to(rb[:, -1:], (h, n_edge // 2)), rb[:, ::-1],
                         jnp.broadcast_to(rb[:, :1], (h, n_edge // 2 + 1))], axis=1)
    f = jnp.roll(f, -(QB - 1), axis=1)
    toep = jnp.tile(f, (1, QB))[:, :QB * (length - 1)].reshape(h, QB, length - 1)[:, :, :width]

    cpb = QB // CHUNK
    r = np.arange(QB)
    c = np.arange(A_BAND)
    tiles = []
    for v in range(A_VARIANTS):
        c0 = v * cpb
        band_start = max(c0 - N_LEFT_CHUNKS, 0)
        delta = (c0 + r[:, None] // CHUNK) - (band_start + c[None, :] // CHUNK)
        ok = (delta >= 0) & (delta <= N_LEFT_CHUNKS)
        col0 = pad - (c0 - band_start) * CHUNK
        tiles.append(jnp.where(ok[None], toep[:, :, col0:col0 + A_BAND], NEG))
    return jnp.stack(tiles, axis=0)


def _ordered_to_float(u):
    bits = jnp.where(u < 0, u ^ jnp.int32(-2147483648), ~u)
    return pltpu.bitcast(bits, F32)


def _for_steps(n, group, body):
    n_main = n // group

    def main(i, carry):
        body(i * group, group)
        return carry
    lax.fori_loop(0, n_main, main, 0)

    def tail(t, carry):
        body(t, 1)
        return carry
    lax.fori_loop(n_main * group, n, tail, 0)


def _attn_b_kernel(q_ref, iq_ref, iw_ref, kb_ref, vb_ref, ik_ref, o_ref,
                   st_ref, hi_ref, lo_ref, sel_ref, mb_ref, s_ref, mrun_ref, acc_ref):
    qi = pl.program_id(1)
    n_tiles = qi + 1
    n_steps = (n_tiles + 1) // 2
    rows = B_HEADS * QB
    i16 = jnp.int16
    i16_min = -2 ** 15

    qs = q_ref[0].reshape(rows, LANES)
    iqs = iq_ref[0].reshape(rows, LANES)
    iw_t = iw_ref[0].T

    key_iota = lax.broadcasted_iota(jnp.int32, (KT, LANES), 0)
    q_lane = _lane_iota((1, LANES))
    key_limit = qi * QB + CHUNK + jnp.where(q_lane >= CHUNK, CHUNK, 0)

    def score_steps(t0, cnt):
        for g in range(cnt):
            off = pl.multiple_of((t0 + g) * KT, KT)
            lg = lax.dot_general(ik_ref[0, pl.ds(off, KT), :], iqs, NT_DIMS,
                                 preferred_element_type=F32)
            sc = None
            for h in range(IDX_HEADS):
                term = jnp.maximum(lg[:, h * QB:(h + 1) * QB], 0.0) * iw_t[h:h + 1, :]
                sc = term if sc is None else sc + term
            sc = jnp.where(key_iota + off < key_limit, sc + 0.0, -jnp.inf)
            st_ref[pl.ds(off, KT), :] = sc
            bits = pltpu.bitcast(sc, jnp.int32)
            key = bits ^ (jnp.right_shift(bits, 31) & 0x7FFFFFFF)
            hi_ref[pl.ds(off, KT), :] = jnp.right_shift(key, 16).astype(i16)
            lo_ref[pl.ds(off, KT), :] = ((key & 0xFFFF) + i16_min).astype(i16)
    _for_steps(n_steps, 2, score_steps)

    sel_ref[0:1, :] = jnp.full((1, LANES), -jnp.inf, F32)
    sel_ref[1:2, :] = jnp.full((1, LANES), -1.0, F32)

    def count(pred):
        def body(t, acc):
            off = pl.multiple_of(t * KT, KT)
            hit = jnp.where(pred(st_ref[pl.ds(off, KT), :], key_iota + off), 1.0, 0.0)
            return acc + hit[:QB] + hit[QB:]

        def body4(i, acc):
            for g in range(4):
                acc = body(i * 4 + g, acc)
            return acc
        acc = lax.fori_loop(0, n_steps // 4, body4, jnp.zeros((QB, LANES), F32))
        acc = lax.fori_loop((n_steps // 4) * 4, n_steps, body, acc)
        return jnp.sum(acc, axis=0, keepdims=True)

    pack = 2 * SUBLANES

    def count16(ref, pred, update=None):
        def body(t, acc):
            off = pl.multiple_of(t * KT, KT)
            x = ref[pl.ds(off, KT), :]
            hit = pred(x)
            if update is not None:
                update(off, hit)
            return acc + jnp.where(hit, jnp.ones((), i16), jnp.zeros((), i16))

        def body4(i, acc):
            for g in range(4):
                acc = body(i * 4 + g, acc)
            return acc
        acc = lax.fori_loop(0, n_steps // 4, body4, jnp.zeros((KT, LANES), i16))
        acc = lax.fori_loop((n_steps // 4) * 4, n_steps, body, acc)
        acc = jnp.sum(acc.reshape(KT // pack, pack, LANES), axis=0)
        return jnp.sum(acc.astype(F32), axis=0, keepdims=True)

    k_sel = float(TOPK_MAX)

    def search16(ref, base):
        def bit_body(i, w):
            trial = w | jnp.left_shift(jnp.int32(1), 15 - i)
            cand = (trial + i16_min).astype(i16)
            cnt = base + count16(ref, lambda x: x >= cand)
            return jnp.where(cnt >= k_sel, trial, w)
        return lax.fori_loop(0, 16, bit_body, jnp.zeros((1, LANES), jnp.int32))

    @pl.when(n_tiles * QB - CHUNK > TOPK_MAX)
    def _search():
        key_hi = search16(hi_ref, 0.0) + i16_min
        hi16 = key_hi.astype(i16)
        n_above = count16(hi_ref, lambda x: x > hi16)

        def keep_matching(off, hit):
            lo_ref[pl.ds(off, KT), :] = jnp.where(hit, lo_ref[pl.ds(off, KT), :],
                                                  jnp.full((), i16_min, i16))
        count16(hi_ref, lambda x: x == hi16, update=keep_matching)
        key = key_hi * 65536 + search16(lo_ref, n_above)

        def publish(thr):
            sel_ref[0:1, :] = thr
            sel_ref[2:3, :] = count(lambda s, k: s > thr)
            sel_ref[3:4, :] = count(lambda s, k: s >= thr)
        publish(pltpu.bitcast(key ^ (jnp.right_shift(key, 31) & 0x7FFFFFFF), F32))

        holds = (sel_ref[2:3, :] < k_sel) & (sel_ref[3:4, :] >= k_sel)

        @pl.when(jnp.min(jnp.where(holds, 1.0, 0.0)) < 1.0)
        def _float_search():
            def bit_body(i, u):
                trial = u | jnp.left_shift(jnp.int32(1), 31 - i)
                cand = _ordered_to_float(trial)
                return jnp.where(count(lambda s, k: s >= cand) >= k_sel, trial, u)
            publish(_ordered_to_float(lax.fori_loop(0, 32, bit_body, jnp.zeros((1, LANES), jnp.int32))))

        thr = sel_ref[0:1, :]
        n_ge = sel_ref[3:4, :]
        need = k_sel - sel_ref[2:3, :]
        sel_ref[1:2, :] = jnp.full((1, LANES), 2.0 ** 30, F32)

        @pl.when(jnp.max(n_ge) > k_sel)
        def _ties():
            def idx_body(i, x):
                trial = x | jnp.left_shift(jnp.int32(1), 11 - i)
                cnt = count(lambda s, k: (s == thr) & (k < trial))
                return jnp.where(cnt < need, trial, x)
            x = lax.fori_loop(0, 12, idx_body, jnp.zeros((1, LANES), jnp.int32))
            sel_ref[1:2, :] = x.astype(F32)

    thr = sel_ref[0:1, :]
    cut = sel_ref[1:2, :].astype(jnp.int32)

    half_iota = key_iota[:QB]

    def mask_steps(t0, cnt):
        for g in range(cnt):
            for half in range(KT // QB):
                off = pl.multiple_of((t0 + g) * KT + half * QB, QB)
                sc = st_ref[pl.ds(off, QB), :]
                tie = jnp.where(half_iota + off <= cut, 0.0, NEG)
                m_t = jnp.where(sc > thr, 0.0, jnp.where(sc == thr, tie, NEG))
                mb_ref[t0 + g, :, half * QB:(half + 1) * QB] = m_t.T
    _for_steps(n_steps, 2, mask_steps)

    mrun_ref[...] = jnp.full((rows, LANES), NEG, F32)

    def pass1_steps(t0, cnt):
        best = None
        for g in range(cnt):
            off = pl.multiple_of((t0 + g) * KT, KT)
            s = lax.dot_general(qs, kb_ref[0, pl.ds(off, KT), :], NT_DIMS,
                                preferred_element_type=F32)
            s = (s.reshape(B_HEADS, QB, KT) + mb_ref[t0 + g][None]).reshape(rows, KT)
            s_ref[t0 + g] = s
            top = jnp.maximum(s[:, :LANES], s[:, LANES:])
            best = top if best is None else jnp.maximum(best, top)
        mrun_ref[...] = jnp.maximum(mrun_ref[...], best)
    _for_steps(n_steps, 2, pass1_steps)

    m_row = jnp.max(mrun_ref[...], axis=1, keepdims=True)
    mrun_ref[...] = jnp.broadcast_to(m_row, (rows, LANES))
    acc_ref[...] = jnp.zeros((rows, LANES), F32)

    def pass2_steps(t0, cnt):
        m_b = mrun_ref[...]
        total = None
        for g in range(cnt):
            off = pl.multiple_of((t0 + g) * KT, KT)
            s = s_ref[t0 + g]
            p = jnp.concatenate([jnp.exp(s[:, :LANES] - m_b), jnp.exp(s[:, LANES:] - m_b)], axis=1)
            pv = jnp.dot(p.astype(BF16), vb_ref[0, pl.ds(off, KT), :], preferred_element_type=F32)
            total = pv if total is None else total + pv
        acc_ref[...] += total
    _for_steps(n_steps, 2, pass2_steps)

    acc = acc_ref[...]
    out = acc / pltpu.roll(acc, B_HEAD_DIM, 1)
    lane = _lane_iota((QB, LANES))
    for g in range(B_WIDTH // LANES):
        lo = out[(2 * g) * QB:(2 * g + 1) * QB]
        hi = pltpu.roll(out[(2 * g + 1) * QB:(2 * g + 2) * QB], B_HEAD_DIM, 1)
        o_ref[0, :, g * LANES:(g + 1) * LANES] = jnp.where(lane < B_HEAD_DIM, lo, hi)


def _attn_b_call(qb, iq, iw, kb, vb, ik, batch, seq):
    n_q = seq // QB
    rows = B_HEADS * QB
    blk = lambda b, i: (b * n_q + i, 0, 0, 0)
    res = lambda b, i: (b, 0, 0)
    return pl.pallas_call(
        _attn_b_kernel,
        grid=(batch, n_q),
        in_specs=[
            pl.BlockSpec((1, B_HEADS, QB, LANES), blk),
            pl.BlockSpec((1, IDX_HEADS, QB, LANES), blk),
            pl.BlockSpec((1, QB, LANES), lambda b, i: (b, i, 0)),
            pl.BlockSpec((1, seq, LANES), res),
            pl.BlockSpec((1, seq, LANES), res),
            pl.BlockSpec((1, seq, LANES), res),
        ],
        out_specs=pl.BlockSpec((1, QB, B_WIDTH), lambda b, i: (b, i, 0)),
        out_shape=jax.ShapeDtypeStruct((batch, seq, B_WIDTH), F32),
        scratch_shapes=[
            pltpu.VMEM((seq, LANES), F32),
            pltpu.VMEM((seq, LANES), jnp.int16),
            pltpu.VMEM((seq, LANES), jnp.int16),
            pltpu.VMEM((SUBLANES, LANES), F32),
            pltpu.VMEM((seq // KT, QB, KT), F32),
            pltpu.VMEM((seq // KT, rows, KT), F32),
            pltpu.VMEM((rows, LANES), F32),
            pltpu.VMEM((rows, LANES), F32),
        ],
        compiler_params=pltpu.CompilerParams(
            dimension_semantics=("arbitrary", "arbitrary"), vmem_limit_bytes=VMEM_LIMIT),
        name="attn_b",
    )(qb, iq, iw, kb, vb, ik)


def _sigmoid(z):
    return 1.0 / (1.0 + jnp.exp(-z))


def _out_kernel(x_ref, g_ref, wg_ref, bm_ref, ya_ref, yb_ref, wa_ref, wb_ref, wo_ref, fg_ref, o_ref):
    x = x_ref[...]
    xn = _rms(x, g_ref[...]).astype(BF16)

    def branch(y_ref, col, w_ref, bias_row):
        gate = jnp.dot(xn, wg_ref[:, col:col + A_WIDTH], preferred_element_type=F32)
        y = (y_ref[...] * (gate * _sigmoid(gate))).astype(BF16)
        pr = jnp.dot(y, w_ref[...], preferred_element_type=F32)
        zcol = 2 * A_WIDTH + bias_row * D_MODEL
        z = jnp.dot(xn, wg_ref[:, zcol:zcol + D_MODEL], preferred_element_type=F32)
        return _sigmoid(z + bm_ref[bias_row:bias_row + 1, :]) * pr

    merged = branch(ya_ref, 0, wa_ref, 0) + branch(yb_ref, A_WIDTH, wb_ref, 1)
    h = x + jnp.dot(merged.astype(BF16), wo_ref[...], preferred_element_type=F32)
    o_ref[...] = _rms(h, fg_ref[...])


def _out_call(x2, gain, w2, bm, ya, yb, wa, wb, wo, fgain):
    m = x2.shape[0]
    tm = OUT_TM
    row = lambda i: (i, 0)
    full = lambda a: pl.BlockSpec(a.shape, lambda i: (0, 0))
    return pl.pallas_call(
        _out_kernel,
        grid=(m // tm,),
        in_specs=[
            pl.BlockSpec((tm, D_MODEL), row), full(gain), full(w2), full(bm),
            pl.BlockSpec((tm, A_WIDTH), row), pl.BlockSpec((tm, B_WIDTH), row),
            full(wa), full(wb), full(wo), full(fgain),
        ],
        out_specs=pl.BlockSpec((tm, D_MODEL), row),
        out_shape=jax.ShapeDtypeStruct((m, D_MODEL), F32),
        compiler_params=pltpu.CompilerParams(
            dimension_semantics=("arbitrary",), vmem_limit_bytes=VMEM_LIMIT),
        name="out",
    )(x2, gain, w2, bm, ya, yb, wa, wb, wo, fgain)


def _rope_tables(seq):
    pos = jnp.arange(seq, dtype=F32)[:, None]

    def pattern(head_dim, rot_dim):
        half = rot_dim // 2
        inv = ROPE_THETA ** (-jnp.arange(half, dtype=F32) / half)
        ang = pos * inv[None, :]
        cos, sin = jnp.cos(ang), jnp.sin(ang)
        ones = jnp.ones((seq, head_dim - rot_dim), F32)
        c = jnp.concatenate([cos, cos, ones], axis=1)
        s = jnp.concatenate([-sin, sin, 0.0 * ones], axis=1)
        reps = LANES // head_dim
        return jnp.tile(c, (1, reps)), jnp.tile(s, (1, reps))

    cq, sq = pattern(B_HEAD_DIM, ROT_DIM_B)
    ci, si = pattern(IDX_DIM, ROT_DIM_IDX)
    return jnp.concatenate([cq, sq, ci, si], axis=1)


def kernel(x, norm_gain, w_in, b_merge, rel_bias, w_branch_a, w_branch_b, w_out, final_norm_gain):
    batch, seq, d = x.shape
    assert d == D_MODEL and seq % PROJ_TM == 0 and norm_gain.shape[0] == 1
    m = batch * seq
    x2 = x.reshape(m, d)
    offsets = [int(o) for o in np.cumsum(SPLIT_SIZES)[:-1]]
    (w_qa, w_ka, w_va, w_ga, w_qb, w_kb, w_vb, w_gb,
     w_iq, w_ik, w_iw, w_za, w_zb) = jnp.split(w_in[0], offsets, axis=1)
    pad = jnp.zeros((d, LANES - IDX_DIM - IDX_HEADS), w_in.dtype)
    w1 = jnp.concatenate([w_qa, w_ka, w_va, w_qb, w_iq, w_kb, w_vb, w_ik, w_iw, pad], axis=1).astype(BF16)
    w2 = jnp.concatenate([w_ga, w_gb, w_za, w_zb], axis=1).astype(BF16)

    qa, ka, va, qb, iq, kb, vb, ik, iw = _proj_call(x2, norm_gain, w1, _rope_tables(seq), batch, seq)

    to3 = lambda a: a.reshape(batch, seq, a.shape[-1])
    ya = _attn_a_call(qa, to3(ka), to3(va), _rel_bias_tiles(rel_bias[0]))
    yb = _attn_b_call(qb, iq, to3(iw), to3(kb), to3(vb), to3(ik), batch, seq)

    out = _out_call(x2, norm_gain, w2, b_merge[0], ya.reshape(m, A_WIDTH), yb.reshape(m, B_WIDTH),
                    w_branch_a[0].astype(BF16), w_branch_b[0].astype(BF16), w_out[0].astype(BF16),
                    final_norm_gain.reshape(1, d))
    return out.reshape(batch, seq, d)
```

```python
import jax
import jax.numpy as jnp
import numpy as np
from jax import lax
from jax.experimental import pallas as pl
from jax.experimental.pallas import tpu as pltpu

D_MODEL = 1024
CHUNK = 64
N_LEFT_CHUNKS = 8
A_HEADS = 8
A_HEAD_DIM = 64
A_WIDTH = A_HEADS * A_HEAD_DIM
REL_CLIP = 256
B_HEADS = 8
B_HEAD_DIM = 64
B_WIDTH = B_HEADS * B_HEAD_DIM
IDX_HEADS = 8
IDX_DIM = 32
TOPK_MAX = 256
ROPE_THETA = 500000.0
ROT_DIM_B = B_HEAD_DIM // 4
ROT_DIM_IDX = IDX_DIM // 4
EPS = 1e-6
NEG = -1e30

SPLIT_SIZES = (
    A_WIDTH, A_WIDTH, A_WIDTH, A_WIDTH,
    B_WIDTH, B_HEAD_DIM, B_HEAD_DIM, B_WIDTH,
    IDX_HEADS * IDX_DIM, IDX_DIM, IDX_HEADS,
    D_MODEL, D_MODEL,
)

LANES = 128
SUBLANES = 8
QB = 128
KT = 2 * QB
KEY_BITS = 32
assert KT == KEY_BITS * SUBLANES
A_BAND = QB + N_LEFT_CHUNKS * CHUNK
A_VARIANTS = N_LEFT_CHUNKS * CHUNK // QB + 1
PROJ_TM = 512
OUT_TM = 256
VMEM_LIMIT = 48 * 1024 * 1024

BF16 = jnp.bfloat16
F32 = jnp.float32
NT_DIMS = (((1,), (1,)), ((), ()))


def _rms(x, g):
    ms = jnp.mean(x * x, axis=-1, keepdims=True)
    return (x * lax.rsqrt(ms + EPS)) * g


def _lane_iota(shape):
    return lax.broadcasted_iota(jnp.int32, shape, len(shape) - 1)


def _rope(xg, c, s, half):
    lane = _lane_iota(xg.shape)
    first = (lane % (2 * half)) < half
    partner = jnp.where(first, pltpu.roll(xg, LANES - half, 1), pltpu.roll(xg, half, 1))
    return xg * c + partner * s


def _proj_kernel(x_ref, g_ref, w_ref, tab_ref,
                 qa_ref, ka_ref, va_ref, qb_ref, iq_ref, kb_ref, vb_ref, ik_ref, iw_ref):
    xn = _rms(x_ref[...], g_ref[...])
    p = jnp.dot(xn.astype(BF16), w_ref[...], preferred_element_type=F32)
    tm = p.shape[0]
    nblk = tm // QB
    lane = _lane_iota((tm, LANES))

    a_scale = A_HEAD_DIM ** -0.5
    b_scale = B_HEAD_DIM ** -0.5
    for g in range(A_WIDTH // LANES):
        xg = p[:, g * LANES:(g + 1) * LANES] * a_scale
        for r in range(LANES // A_HEAD_DIM):
            own = (lane // A_HEAD_DIM) == r
            piece = jnp.where(own, xg, 0.0).astype(BF16)
            for j in range(nblk):
                qa_ref[j, g * (LANES // A_HEAD_DIM) + r] = piece[j * QB:(j + 1) * QB]
    ka_ref[...] = p[:, A_WIDTH:2 * A_WIDTH].astype(BF16)
    va_ref[...] = p[:, 2 * A_WIDTH:3 * A_WIDTH].astype(BF16)

    cq = tab_ref[:, 0:LANES]
    sq = tab_ref[:, LANES:2 * LANES]
    ci = tab_ref[:, 2 * LANES:3 * LANES]
    si = tab_ref[:, 3 * LANES:4 * LANES]

    base = 3 * A_WIDTH
    for g in range(B_WIDTH // LANES):
        xg = _rope(p[:, base + g * LANES: base + (g + 1) * LANES], cq, sq, ROT_DIM_B // 2) * b_scale
        per_group = LANES // B_HEAD_DIM
        for r in range(per_group):
            h = g * per_group + r
            piece = xg if r == 0 else pltpu.roll(xg, LANES - r * B_HEAD_DIM, 1)
            piece = jnp.where(lane < B_HEAD_DIM, piece, 0.0).astype(BF16)
            for j in range(nblk):
                qb_ref[j, h] = piece[j * QB:(j + 1) * QB]

    base = 3 * A_WIDTH + B_WIDTH
    for g in range(IDX_HEADS * IDX_DIM // LANES):
        xg = _rope(p[:, base + g * LANES: base + (g + 1) * LANES], ci, si, ROT_DIM_IDX // 2)
        per_group = LANES // IDX_DIM
        for r in range(per_group):
            h = g * per_group + r
            piece = xg if r == 0 else pltpu.roll(xg, LANES - r * IDX_DIM, 1)
            piece = jnp.where(lane < IDX_DIM, piece, 0.0).astype(BF16)
            for j in range(nblk):
                iq_ref[j, h] = piece[j * QB:(j + 1) * QB]

    base = 3 * A_WIDTH + B_WIDTH + IDX_HEADS * IDX_DIM
    is_k = lane < B_HEAD_DIM
    kv = _rope(p[:, base:base + LANES], jnp.where(is_k, cq, 1.0), jnp.where(is_k, sq, 0.0),
               ROT_DIM_B // 2)
    kb_ref[...] = jnp.where(is_k, kv, 0.0).astype(BF16)
    vb_ref[...] = jnp.where(is_k, pltpu.roll(kv, LANES - B_HEAD_DIM, 1), 1.0).astype(BF16)

    base = base + LANES
    is_ik = lane < IDX_DIM
    kw = _rope(p[:, base:base + LANES], jnp.where(is_ik, ci, 1.0), jnp.where(is_ik, si, 0.0),
               ROT_DIM_IDX // 2)
    ik_ref[...] = jnp.where(is_ik, kw, 0.0).astype(BF16)
    iw_scale = IDX_HEADS ** -0.5 * IDX_DIM ** -0.5
    iw_ref[...] = pltpu.roll(kw, LANES - IDX_DIM, 1) * iw_scale


def _proj_call(x2, gain, w1, tab, batch, seq):
    m = x2.shape[0]
    tm = PROJ_TM
    n_s = seq // tm
    nblk = tm // QB
    row = lambda s, b: (b * n_s + s, 0)
    blk4 = lambda s, b: (b * n_s + s, 0, 0, 0)
    wide = lambda width, dtype: jax.ShapeDtypeStruct((m, width), dtype)
    stacked = jax.ShapeDtypeStruct((m // QB, B_HEADS, QB, LANES), BF16)
    return pl.pallas_call(
        _proj_kernel,
        grid=(n_s, batch),
        in_specs=[
            pl.BlockSpec((tm, D_MODEL), row),
            pl.BlockSpec((1, D_MODEL), lambda s, b: (0, 0)),
            pl.BlockSpec(w1.shape, lambda s, b: (0, 0)),
            pl.BlockSpec((tm, 4 * LANES), lambda s, b: (s, 0)),
        ],
        out_specs=[
            pl.BlockSpec((nblk, A_HEADS, QB, LANES), blk4),
            pl.BlockSpec((tm, A_WIDTH), row),
            pl.BlockSpec((tm, A_WIDTH), row),
            pl.BlockSpec((nblk, B_HEADS, QB, LANES), blk4),
            pl.BlockSpec((nblk, IDX_HEADS, QB, LANES), blk4),
            pl.BlockSpec((tm, LANES), row),
            pl.BlockSpec((tm, LANES), row),
            pl.BlockSpec((tm, LANES), row),
            pl.BlockSpec((tm, LANES), row),
        ],
        out_shape=[
            stacked, wide(A_WIDTH, BF16), wide(A_WIDTH, BF16),
            stacked, stacked,
            wide(LANES, BF16), wide(LANES, BF16), wide(LANES, BF16), wide(LANES, F32),
        ],
        compiler_params=pltpu.CompilerParams(
            dimension_semantics=("arbitrary", "arbitrary"), vmem_limit_bytes=VMEM_LIMIT),
        name="proj",
    )(x2, gain, w1, tab)


def _attn_a_kernel(q_ref, k_ref, v_ref, bias_ref, o_ref, s_ref, p_ref, rinv_ref):
    qi = pl.program_id(1)
    start = pl.multiple_of(jnp.maximum(qi * QB - N_LEFT_CHUNKS * CHUNK, 0), QB)
    lane = _lane_iota((QB, LANES))
    per_group = LANES // A_HEAD_DIM
    group_cols = lambda h: slice((h // per_group) * LANES, (h // per_group + 1) * LANES)

    for h in range(A_HEADS):
        kg = k_ref[0, pl.ds(start, A_BAND), group_cols(h)]
        s_ref[h] = lax.dot_general(q_ref[0, h], kg, NT_DIMS, preferred_element_type=F32) + bias_ref[0, h]
    for h in range(A_HEADS):
        s = s_ref[h]
        e = jnp.exp(s - jnp.max(s, axis=-1, keepdims=True))
        p_ref[h] = e.astype(BF16)
        rinv_ref[h] = jnp.broadcast_to(1.0 / jnp.sum(e, axis=-1, keepdims=True), (QB, LANES))
    for g in range(A_WIDTH // LANES):
        out = None
        for r in range(per_group):
            h = g * per_group + r
            vg = v_ref[0, pl.ds(start, A_BAND), group_cols(h)]
            o = jnp.dot(p_ref[h], vg, preferred_element_type=F32) * rinv_ref[h]
            out = o if out is None else jnp.where((lane // A_HEAD_DIM) == r, o, out)
        o_ref[0, :, g * LANES:(g + 1) * LANES] = out


def _attn_a_call(qa, ka, va, bias):
    batch, seq, _ = ka.shape
    n_q = seq // QB
    return pl.pallas_call(
        _attn_a_kernel,
        grid=(batch, n_q),
        in_specs=[
            pl.BlockSpec((1, A_HEADS, QB, LANES), lambda b, i: (b * n_q + i, 0, 0, 0)),
            pl.BlockSpec((1, seq, A_WIDTH), lambda b, i: (b, 0, 0)),
            pl.BlockSpec((1, seq, A_WIDTH), lambda b, i: (b, 0, 0)),
            pl.BlockSpec((1, A_HEADS, QB, A_BAND),
                         lambda b, i: (jnp.minimum(i, A_VARIANTS - 1), 0, 0, 0)),
        ],
        out_specs=pl.BlockSpec((1, QB, A_WIDTH), lambda b, i: (b, i, 0)),
        out_shape=jax.ShapeDtypeStruct((batch, seq, A_WIDTH), F32),
        scratch_shapes=[
            pltpu.VMEM((A_HEADS, QB, A_BAND), F32),
            pltpu.VMEM((A_HEADS, QB, A_BAND), BF16),
            pltpu.VMEM((A_HEADS, QB, LANES), F32),
        ],
        compiler_params=pltpu.CompilerParams(
            dimension_semantics=("arbitrary", "arbitrary"), vmem_limit_bytes=VMEM_LIMIT),
        name="attn_a",
    )(qa, ka, va, bias)


def _rel_bias_tiles(rel_bias):
    pad = N_LEFT_CHUNKS * CHUNK
    width = pad + A_BAND
    length = width + QB
    n_edge = length - 1 - (2 * REL_CLIP + 1)
    assert n_edge % 2 == 0
    rb = rel_bias.astype(F32)
    h = rb.shape[0]
    f = jnp.concatenate([jnp.broadcast_to(rb[:, -1:], (h, n_edge // 2)), rb[:, ::-1],
                         jnp.broadcast_to(rb[:, :1], (h, n_edge // 2 + 1))], axis=1)
    f = jnp.roll(f, -(QB - 1), axis=1)
    toep = jnp.tile(f, (1, QB))[:, :QB * (length - 1)].reshape(h, QB, length - 1)[:, :, :width]

    cpb = QB // CHUNK
    r = np.arange(QB)
    c = np.arange(A_BAND)
    tiles = []
    for v in range(A_VARIANTS):
        c0 = v * cpb
        band_start = max(c0 - N_LEFT_CHUNKS, 0)
        delta = (c0 + r[:, None] // CHUNK) - (band_start + c[None, :] // CHUNK)
        ok = (delta >= 0) & (delta <= N_LEFT_CHUNKS)
        col0 = pad - (c0 - band_start) * CHUNK
        tiles.append(jnp.where(ok[None], toep[:, :, col0:col0 + A_BAND], NEG))
    return jnp.stack(tiles, axis=0)


def _ordered_to_float(u):
    bits = jnp.where(u < 0, u ^ jnp.int32(-2147483648), ~u)
    return pltpu.bitcast(bits, F32)


def _for_steps(n, group, body):
    n_main = n // group

    def main(i, carry):
        body(i * group, group)
        return carry
    lax.fori_loop(0, n_main, main, 0)

    def tail(t, carry):
        body(t, 1)
        return carry
    lax.fori_loop(n_main * group, n, tail, 0)


def _bit_transpose(words):
    a = list(words)
    j, mask = KEY_BITS // 2, 0x0000FFFF
    while j:
        k = 0
        while k < KEY_BITS:
            t = (a[k] ^ lax.shift_right_logical(a[k + j], jnp.int32(j))) & mask
            a[k] = a[k] ^ t
            a[k + j] = a[k + j] ^ jnp.left_shift(t, jnp.int32(j))
            k = (k + j + 1) & ~j
        j >>= 1
        mask ^= (mask << j) & 0xFFFFFFFF
    return a


def _attn_b_kernel(q_ref, iq_ref, iw_ref, kb_ref, vb_ref, ik_ref, o_ref,
                   st_ref, plane_ref, sel_ref, mb_ref, s_ref, mrun_ref, acc_ref):
    qi = pl.program_id(1)
    n_tiles = qi + 1
    n_steps = (n_tiles + 1) // 2
    n_max = st_ref.shape[0] // KT
    rows = B_HEADS * QB
    int_min = jnp.int32(-2 ** 31)

    qs = q_ref[0].reshape(rows, LANES)
    iqs = iq_ref[0].reshape(rows, LANES)
    iw_t = iw_ref[0].T

    key_iota = lax.broadcasted_iota(jnp.int32, (KT, LANES), 0)
    q_lane = _lane_iota((1, LANES))
    key_limit = qi * QB + CHUNK + jnp.where(q_lane >= CHUNK, CHUNK, 0)

    def score_steps(t0, cnt):
        for g in range(cnt):
            off = pl.multiple_of((t0 + g) * KT, KT)
            lg = lax.dot_general(ik_ref[0, pl.ds(off, KT), :], iqs, NT_DIMS,
                                 preferred_element_type=F32)
            sc = None
            for h in range(IDX_HEADS):
                term = jnp.maximum(lg[:, h * QB:(h + 1) * QB], 0.0) * iw_t[h:h + 1, :]
                sc = term if sc is None else sc + term
            sc = jnp.where(key_iota + off < key_limit, sc + 0.0, -jnp.inf)
            st_ref[pl.ds(off, KT), :] = sc
            bits = pltpu.bitcast(sc, jnp.int32)
            key = bits ^ (jnp.right_shift(bits, 31) & 0x7FFFFFFF) ^ int_min
            planes = _bit_transpose([key[j * SUBLANES:(j + 1) * SUBLANES] for j in range(KEY_BITS)])
            row0 = pl.multiple_of((t0 + g) * SUBLANES, SUBLANES)
            for i in range(KEY_BITS):
                plane_ref[i, pl.ds(row0, SUBLANES), :] = planes[i]
    _for_steps(n_steps, 2, score_steps)

    sel_ref[0:1, :] = jnp.full((1, LANES), -jnp.inf, F32)
    sel_ref[1:2, :] = jnp.full((1, LANES), -1.0, F32)

    def count(pred):
        def body(t, acc):
            off = pl.multiple_of(t * KT, KT)
            hit = jnp.where(pred(st_ref[pl.ds(off, KT), :], key_iota + off), 1.0, 0.0)
            return acc + hit[:QB] + hit[QB:]

        def body4(i, acc):
            for g in range(4):
                acc = body(i * 4 + g, acc)
            return acc
        acc = lax.fori_loop(0, n_steps // 4, body4, jnp.zeros((QB, LANES), F32))
        acc = lax.fori_loop((n_steps // 4) * 4, n_steps, body, acc)
        return jnp.sum(acc, axis=0, keepdims=True)

    k_sel = float(TOPK_MAX)

    @pl.when(n_tiles * QB - CHUNK > TOPK_MAX)
    def _search():
        step_rows = lax.broadcasted_iota(jnp.int32, (n_max * SUBLANES, LANES), 0)
        alive0 = jnp.where(step_rows < n_steps * SUBLANES, -1, 0)

        def bit_body(i, carry):
            alive, rem, key = carry
            ones = alive & plane_ref[i]
            n_ones = jnp.sum(lax.population_count(ones), axis=0, keepdims=True)
            take = n_ones >= rem
            alive = jnp.where(take, ones, alive ^ ones)
            rem = jnp.where(take, rem, rem - n_ones)
            key = key | jnp.where(take, jnp.left_shift(jnp.int32(1), KEY_BITS - 1 - i), 0)
            return alive, rem, key
        _, _, key = lax.fori_loop(0, KEY_BITS, bit_body,
                                  (alive0, jnp.full((1, LANES), TOPK_MAX, jnp.int32),
                                   jnp.zeros((1, LANES), jnp.int32)))
        key = key ^ int_min

        def publish(thr):
            sel_ref[0:1, :] = thr
            sel_ref[2:3, :] = count(lambda s, k: s > thr)
            sel_ref[3:4, :] = count(lambda s, k: s >= thr)
        publish(pltpu.bitcast(key ^ (jnp.right_shift(key, 31) & 0x7FFFFFFF), F32))

        holds = (sel_ref[2:3, :] < k_sel) & (sel_ref[3:4, :] >= k_sel)

        @pl.when(jnp.min(jnp.where(holds, 1.0, 0.0)) < 1.0)
        def _float_search():
            def bit_body(i, u):
                trial = u | jnp.left_shift(jnp.int32(1), 31 - i)
                cand = _ordered_to_float(trial)
                return jnp.where(count(lambda s, k: s >= cand) >= k_sel, trial, u)
            publish(_ordered_to_float(lax.fori_loop(0, 32, bit_body, jnp.zeros((1, LANES), jnp.int32))))

        thr = sel_ref[0:1, :]
        n_ge = sel_ref[3:4, :]
        need = k_sel - sel_ref[2:3, :]
        sel_ref[1:2, :] = jnp.full((1, LANES), 2.0 ** 30, F32)

        @pl.when(jnp.max(n_ge) > k_sel)
        def _ties():
            def idx_body(i, x):
                trial = x | jnp.left_shift(jnp.int32(1), 11 - i)
                cnt = count(lambda s, k: (s == thr) & (k < trial))
                return jnp.where(cnt < need, trial, x)
            x = lax.fori_loop(0, 12, idx_body, jnp.zeros((1, LANES), jnp.int32))
            sel_ref[1:2, :] = x.astype(F32)

    thr = sel_ref[0:1, :]
    cut = sel_ref[1:2, :].astype(jnp.int32)

    half_iota = key_iota[:QB]

    def mask_steps(t0, cnt):
        for g in range(cnt):
            for half in range(KT // QB):
                off = pl.multiple_of((t0 + g) * KT + half * QB, QB)
                sc = st_ref[pl.ds(off, QB), :]
                tie = jnp.where(half_iota + off <= cut, 0.0, NEG)
                m_t = jnp.where(sc > thr, 0.0, jnp.where(sc == thr, tie, NEG))
                mb_ref[t0 + g, :, half * QB:(half + 1) * QB] = m_t.T
    _for_steps(n_steps, 2, mask_steps)

    mrun_ref[...] = jnp.full((rows, LANES), NEG, F32)

    def pass1_steps(t0, cnt):
        best = None
        for g in range(cnt):
            off = pl.multiple_of((t0 + g) * KT, KT)
            s = lax.dot_general(qs, kb_ref[0, pl.ds(off, KT), :], NT_DIMS,
                                preferred_element_type=F32)
            s = (s.reshape(B_HEADS, QB, KT) + mb_ref[t0 + g][None]).reshape(rows, KT)
            s_ref[t0 + g] = s
            top = jnp.maximum(s[:, :LANES], s[:, LANES:])
            best = top if best is None else jnp.maximum(best, top)
        mrun_ref[...] = jnp.maximum(mrun_ref[...], best)
    _for_steps(n_steps, 2, pass1_steps)

    m_row = jnp.max(mrun_ref[...], axis=1, keepdims=True)
    mrun_ref[...] = jnp.broadcast_to(m_row, (rows, LANES))
    acc_ref[...] = jnp.zeros((rows, LANES), F32)

    def pass2_steps(t0, cnt):
        m_b = mrun_ref[...]
        total = None
        for g in range(cnt):
            off = pl.multiple_of((t0 + g) * KT, KT)
            s = s_ref[t0 + g]
            p = jnp.concatenate([jnp.exp(s[:, :LANES] - m_b), jnp.exp(s[:, LANES:] - m_b)], axis=1)
            pv = jnp.dot(p.astype(BF16), vb_ref[0, pl.ds(off, KT), :], preferred_element_type=F32)
            total = pv if total is None else total + pv
        acc_ref[...] += total
    _for_steps(n_steps, 2, pass2_steps)

    acc = acc_ref[...]
    out = acc / pltpu.roll(acc, B_HEAD_DIM, 1)
    lane = _lane_iota((QB, LANES))
    for g in range(B_WIDTH // LANES):
        lo = out[(2 * g) * QB:(2 * g + 1) * QB]
        hi = pltpu.roll(out[(2 * g + 1) * QB:(2 * g + 2) * QB], B_HEAD_DIM, 1)
        o_ref[0, :, g * LANES:(g + 1) * LANES] = jnp.where(lane < B_HEAD_DIM, lo, hi)


def _attn_b_call(qb, iq, iw, kb, vb, ik, batch, seq):
    n_q = seq // QB
    rows = B_HEADS * QB
    blk = lambda b, i: (b * n_q + i, 0, 0, 0)
    res = lambda b, i: (b, 0, 0)
    return pl.pallas_call(
        _attn_b_kernel,
        grid=(batch, n_q),
        in_specs=[
            pl.BlockSpec((1, B_HEADS, QB, LANES), blk),
            pl.BlockSpec((1, IDX_HEADS, QB, LANES), blk),
            pl.BlockSpec((1, QB, LANES), lambda b, i: (b, i, 0)),
            pl.BlockSpec((1, seq, LANES), res),
            pl.BlockSpec((1, seq, LANES), res),
            pl.BlockSpec((1, seq, LANES), res),
        ],
        out_specs=pl.BlockSpec((1, QB, B_WIDTH), lambda b, i: (b, i, 0)),
        out_shape=jax.ShapeDtypeStruct((batch, seq, B_WIDTH), F32),
        scratch_shapes=[
            pltpu.VMEM((seq, LANES), F32),
            pltpu.VMEM((KEY_BITS, seq // KT * SUBLANES, LANES), jnp.int32),
            pltpu.VMEM((SUBLANES, LANES), F32),
            pltpu.VMEM((seq // KT, QB, KT), F32),
            pltpu.VMEM((seq // KT, rows, KT), F32),
            pltpu.VMEM((rows, LANES), F32),
            pltpu.VMEM((rows, LANES), F32),
        ],
        compiler_params=pltpu.CompilerParams(
            dimension_semantics=("arbitrary", "arbitrary"), vmem_limit_bytes=VMEM_LIMIT),
        name="attn_b",
    )(qb, iq, iw, kb, vb, ik)


def _sigmoid(z):
    return 1.0 / (1.0 + jnp.exp(-z))


def _out_kernel(x_ref, g_ref, wg_ref, bm_ref, ya_ref, yb_ref, wa_ref, wb_ref, wo_ref, fg_ref, o_ref):
    x = x_ref[...]
    xn = _rms(x, g_ref[...]).astype(BF16)

    def branch(y_ref, col, w_ref, bias_row):
        gate = jnp.dot(xn, wg_ref[:, col:col + A_WIDTH], preferred_element_type=F32)
        y = (y_ref[...] * (gate * _sigmoid(gate))).astype(BF16)
        pr = jnp.dot(y, w_ref[...], preferred_element_type=F32)
        zcol = 2 * A_WIDTH + bias_row * D_MODEL
        z = jnp.dot(xn, wg_ref[:, zcol:zcol + D_MODEL], preferred_element_type=F32)
        return _sigmoid(z + bm_ref[bias_row:bias_row + 1, :]) * pr

    merged = branch(ya_ref, 0, wa_ref, 0) + branch(yb_ref, A_WIDTH, wb_ref, 1)
    h = x + jnp.dot(merged.astype(BF16), wo_ref[...], preferred_element_type=F32)
    o_ref[...] = _rms(h, fg_ref[...])


def _out_call(x2, gain, w2, bm, ya, yb, wa, wb, wo, fgain):
    m = x2.shape[0]
    tm = OUT_TM
    row = lambda i: (i, 0)
    full = lambda a: pl.BlockSpec(a.shape, lambda i: (0, 0))
    return pl.pallas_call(
        _out_kernel,
        grid=(m // tm,),
        in_specs=[
            pl.BlockSpec((tm, D_MODEL), row), full(gain), full(w2), full(bm),
            pl.BlockSpec((tm, A_WIDTH), row), pl.BlockSpec((tm, B_WIDTH), row),
            full(wa), full(wb), full(wo), full(fgain),
        ],
        out_specs=pl.BlockSpec((tm, D_MODEL), row),
        out_shape=jax.ShapeDtypeStruct((m, D_MODEL), F32),
        compiler_params=pltpu.CompilerParams(
            dimension_semantics=("arbitrary",), vmem_limit_bytes=VMEM_LIMIT),
        name="out",
    )(x2, gain, w2, bm, ya, yb, wa, wb, wo, fgain)


def _rope_tables(seq):
    pos = jnp.arange(seq, dtype=F32)[:, None]

    def pattern(head_dim, rot_dim):
        half = rot_dim // 2
        inv = ROPE_THETA ** (-jnp.arange(half, dtype=F32) / half)
        ang = pos * inv[None, :]
        cos, sin = jnp.cos(ang), jnp.sin(ang)
        ones = jnp.ones((seq, head_dim - rot_dim), F32)
        c = jnp.concatenate([cos, cos, ones], axis=1)
        s = jnp.concatenate([-sin, sin, 0.0 * ones], axis=1)
        reps = LANES // head_dim
        return jnp.tile(c, (1, reps)), jnp.tile(s, (1, reps))

    cq, sq = pattern(B_HEAD_DIM, ROT_DIM_B)
    ci, si = pattern(IDX_DIM, ROT_DIM_IDX)
    return jnp.concatenate([cq, sq, ci, si], axis=1)


def kernel(x, norm_gain, w_in, b_merge, rel_bias, w_branch_a, w_branch_b, w_out, final_norm_gain):
    batch, seq, d = x.shape
    assert d == D_MODEL and seq % PROJ_TM == 0 and norm_gain.shape[0] == 1
    m = batch * seq
    x2 = x.reshape(m, d)
    offsets = [int(o) for o in np.cumsum(SPLIT_SIZES)[:-1]]
    (w_qa, w_ka, w_va, w_ga, w_qb, w_kb, w_vb, w_gb,
     w_iq, w_ik, w_iw, w_za, w_zb) = jnp.split(w_in[0], offsets, axis=1)
    pad = jnp.zeros((d, LANES - IDX_DIM - IDX_HEADS), w_in.dtype)
    w1 = jnp.concatenate([w_qa, w_ka, w_va, w_qb, w_iq, w_kb, w_vb, w_ik, w_iw, pad], axis=1).astype(BF16)
    w2 = jnp.concatenate([w_ga, w_gb, w_za, w_zb], axis=1).astype(BF16)

    qa, ka, va, qb, iq, kb, vb, ik, iw = _proj_call(x2, norm_gain, w1, _rope_tables(seq), batch, seq)

    to3 = lambda a: a.reshape(batch, seq, a.shape[-1])
    ya = _attn_a_call(qa, to3(ka), to3(va), _rel_bias_tiles(rel_bias[0]))
    yb = _attn_b_call(qb, iq, to3(iw), to3(kb), to3(vb), to3(ik), batch, seq)

    out = _out_call(x2, norm_gain, w2, b_merge[0], ya.reshape(m, A_WIDTH), yb.reshape(m, B_WIDTH),
                    w_branch_a[0].astype(BF16), w_branch_b[0].astype(BF16), w_out[0].astype(BF16),
                    final_norm_gain.reshape(1, d))
    return out.reshape(batch, seq, d)
```

```python
import jax
import jax.numpy as jnp
import numpy as np
from jax import lax
from jax.experimental import pallas as pl
from jax.experimental.pallas import tpu as pltpu

D_MODEL = 1024
CHUNK = 64
N_LEFT_CHUNKS = 8
A_HEADS = 8
A_HEAD_DIM = 64
A_WIDTH = A_HEADS * A_HEAD_DIM
REL_CLIP = 256
B_HEADS = 8
B_HEAD_DIM = 64
B_WIDTH = B_HEADS * B_HEAD_DIM
IDX_HEADS = 8
IDX_DIM = 32
TOPK_MAX = 256
ROPE_THETA = 500000.0
ROT_DIM_B = B_HEAD_DIM // 4
ROT_DIM_IDX = IDX_DIM // 4
EPS = 1e-6
NEG = -1e30

SPLIT_SIZES = (
    A_WIDTH, A_WIDTH, A_WIDTH, A_WIDTH,
    B_WIDTH, B_HEAD_DIM, B_HEAD_DIM, B_WIDTH,
    IDX_HEADS * IDX_DIM, IDX_DIM, IDX_HEADS,
    D_MODEL, D_MODEL,
)

LANES = 128
SUBLANES = 8
QB = 128
KT = 2 * QB
KEY_BITS = 32
assert KT == KEY_BITS * SUBLANES
A_BAND = QB + N_LEFT_CHUNKS * CHUNK
A_VARIANTS = N_LEFT_CHUNKS * CHUNK // QB + 1
PROJ_TM = 512
OUT_TM = 256
VMEM_LIMIT = 48 * 1024 * 1024

BF16 = jnp.bfloat16
F32 = jnp.float32
NT_DIMS = (((1,), (1,)), ((), ()))


def _rms(x, g):
    ms = jnp.mean(x * x, axis=-1, keepdims=True)
    return (x * lax.rsqrt(ms + EPS)) * g


def _lane_iota(shape):
    return lax.broadcasted_iota(jnp.int32, shape, len(shape) - 1)


def _rope(xg, c, s, half):
    lane = _lane_iota(xg.shape)
    first = (lane % (2 * half)) < half
    partner = jnp.where(first, pltpu.roll(xg, LANES - half, 1), pltpu.roll(xg, half, 1))
    return xg * c + partner * s


def _proj_kernel(x_ref, g_ref, w_ref, tab_ref,
                 qa_ref, ka_ref, va_ref, qb_ref, iq_ref, kb_ref, vb_ref, ik_ref, iw_ref):
    xn = _rms(x_ref[...], g_ref[...])
    p = jnp.dot(xn.astype(BF16), w_ref[...], preferred_element_type=F32)
    tm = p.shape[0]
    nblk = tm // QB
    lane = _lane_iota((tm, LANES))

    a_scale = A_HEAD_DIM ** -0.5
    b_scale = B_HEAD_DIM ** -0.5
    for g in range(A_WIDTH // LANES):
        xg = p[:, g * LANES:(g + 1) * LANES] * a_scale
        for r in range(LANES // A_HEAD_DIM):
            own = (lane // A_HEAD_DIM) == r
            piece = jnp.where(own, xg, 0.0).astype(BF16)
            for j in range(nblk):
                qa_ref[j, g * (LANES // A_HEAD_DIM) + r] = piece[j * QB:(j + 1) * QB]
    ka_ref[...] = p[:, A_WIDTH:2 * A_WIDTH].astype(BF16)
    va_ref[...] = p[:, 2 * A_WIDTH:3 * A_WIDTH].astype(BF16)

    cq = tab_ref[:, 0:LANES]
    sq = tab_ref[:, LANES:2 * LANES]
    ci = tab_ref[:, 2 * LANES:3 * LANES]
    si = tab_ref[:, 3 * LANES:4 * LANES]

    base = 3 * A_WIDTH
    for g in range(B_WIDTH // LANES):
        xg = _rope(p[:, base + g * LANES: base + (g + 1) * LANES], cq, sq, ROT_DIM_B // 2) * b_scale
        per_group = LANES // B_HEAD_DIM
        for r in range(per_group):
            h = g * per_group + r
            piece = xg if r == 0 else pltpu.roll(xg, LANES - r * B_HEAD_DIM, 1)
            piece = jnp.where(lane < B_HEAD_DIM, piece, 0.0).astype(BF16)
            for j in range(nblk):
                qb_ref[j, h] = piece[j * QB:(j + 1) * QB]

    base = 3 * A_WIDTH + B_WIDTH
    for g in range(IDX_HEADS * IDX_DIM // LANES):
        xg = _rope(p[:, base + g * LANES: base + (g + 1) * LANES], ci, si, ROT_DIM_IDX // 2)
        per_group = LANES // IDX_DIM
        for r in range(per_group):
            h = g * per_group + r
            piece = xg if r == 0 else pltpu.roll(xg, LANES - r * IDX_DIM, 1)
            piece = jnp.where(lane < IDX_DIM, piece, 0.0).astype(BF16)
            for j in range(nblk):
                iq_ref[j, h] = piece[j * QB:(j + 1) * QB]

    base = 3 * A_WIDTH + B_WIDTH + IDX_HEADS * IDX_DIM
    is_k = lane < B_HEAD_DIM
    kv = _rope(p[:, base:base + LANES], jnp.where(is_k, cq, 1.0), jnp.where(is_k, sq, 0.0),
               ROT_DIM_B // 2)
    kb_ref[...] = jnp.where(is_k, kv, 0.0).astype(BF16)
    vb_ref[...] = jnp.where(is_k, pltpu.roll(kv, LANES - B_HEAD_DIM, 1), 1.0).astype(BF16)

    base = base + LANES
    is_ik = lane < IDX_DIM
    kw = _rope(p[:, base:base + LANES], jnp.where(is_ik, ci, 1.0), jnp.where(is_ik, si, 0.0),
               ROT_DIM_IDX // 2)
    ik_ref[...] = jnp.where(is_ik, kw, 0.0).astype(BF16)
    iw_scale = IDX_HEADS ** -0.5 * IDX_DIM ** -0.5
    iw_ref[...] = pltpu.roll(kw, LANES - IDX_DIM, 1) * iw_scale


def _proj_call(x2, gain, w1, tab, batch, seq):
    m = x2.shape[0]
    tm = PROJ_TM
    n_s = seq // tm
    nblk = tm // QB
    row = lambda s, b: (b * n_s + s, 0)
    blk4 = lambda s, b: (b * n_s + s, 0, 0, 0)
    wide = lambda width, dtype: jax.ShapeDtypeStruct((m, width), dtype)
    stacked = jax.ShapeDtypeStruct((m // QB, B_HEADS, QB, LANES), BF16)
    return pl.pallas_call(
        _proj_kernel,
        grid=(n_s, batch),
        in_specs=[
            pl.BlockSpec((tm, D_MODEL), row),
            pl.BlockSpec((1, D_MODEL), lambda s, b: (0, 0)),
            pl.BlockSpec(w1.shape, lambda s, b: (0, 0)),
            pl.BlockSpec((tm, 4 * LANES), lambda s, b: (s, 0)),
        ],
        out_specs=[
            pl.BlockSpec((nblk, A_HEADS, QB, LANES), blk4),
            pl.BlockSpec((tm, A_WIDTH), row),
            pl.BlockSpec((tm, A_WIDTH), row),
            pl.BlockSpec((nblk, B_HEADS, QB, LANES), blk4),
            pl.BlockSpec((nblk, IDX_HEADS, QB, LANES), blk4),
            pl.BlockSpec((tm, LANES), row),
            pl.BlockSpec((tm, LANES), row),
            pl.BlockSpec((tm, LANES), row),
            pl.BlockSpec((tm, LANES), row),
        ],
        out_shape=[
            stacked, wide(A_WIDTH, BF16), wide(A_WIDTH, BF16),
            stacked, stacked,
            wide(LANES, BF16), wide(LANES, BF16), wide(LANES, BF16), wide(LANES, F32),
        ],
        compiler_params=pltpu.CompilerParams(
            dimension_semantics=("arbitrary", "arbitrary"), vmem_limit_bytes=VMEM_LIMIT),
        name="proj",
    )(x2, gain, w1, tab)


def _attn_a_kernel(q_ref, k_ref, v_ref, bias_ref, o_ref, s_ref, p_ref, rinv_ref):
    qi = pl.program_id(1)
    start = pl.multiple_of(jnp.maximum(qi * QB - N_LEFT_CHUNKS * CHUNK, 0), QB)
    lane = _lane_iota((QB, LANES))
    per_group = LANES // A_HEAD_DIM
    group_cols = lambda h: slice((h // per_group) * LANES, (h // per_group + 1) * LANES)

    for h in range(A_HEADS):
        kg = k_ref[0, pl.ds(start, A_BAND), group_cols(h)]
        s_ref[h] = lax.dot_general(q_ref[0, h], kg, NT_DIMS, preferred_element_type=F32) + bias_ref[0, h]
    for h in range(A_HEADS):
        s = s_ref[h]
        e = jnp.exp(s - jnp.max(s, axis=-1, keepdims=True))
        p_ref[h] = e.astype(BF16)
        rinv_ref[h] = jnp.broadcast_to(1.0 / jnp.sum(e, axis=-1, keepdims=True), (QB, LANES))
    for g in range(A_WIDTH // LANES):
        out = None
        for r in range(per_group):
            h = g * per_group + r
            vg = v_ref[0, pl.ds(start, A_BAND), group_cols(h)]
            o = jnp.dot(p_ref[h], vg, preferred_element_type=F32) * rinv_ref[h]
            out = o if out is None else jnp.where((lane // A_HEAD_DIM) == r, o, out)
        o_ref[0, :, g * LANES:(g + 1) * LANES] = out


def _attn_a_call(qa, ka, va, bias):
    batch, seq, _ = ka.shape
    n_q = seq // QB
    return pl.pallas_call(
        _attn_a_kernel,
        grid=(batch, n_q),
        in_specs=[
            pl.BlockSpec((1, A_HEADS, QB, LANES), lambda b, i: (b * n_q + i, 0, 0, 0)),
            pl.BlockSpec((1, seq, A_WIDTH), lambda b, i: (b, 0, 0)),
            pl.BlockSpec((1, seq, A_WIDTH), lambda b, i: (b, 0, 0)),
            pl.BlockSpec((1, A_HEADS, QB, A_BAND),
                         lambda b, i: (jnp.minimum(i, A_VARIANTS - 1), 0, 0, 0)),
        ],
        out_specs=pl.BlockSpec((1, QB, A_WIDTH), lambda b, i: (b, i, 0)),
        out_shape=jax.ShapeDtypeStruct((batch, seq, A_WIDTH), F32),
        scratch_shapes=[
            pltpu.VMEM((A_HEADS, QB, A_BAND), F32),
            pltpu.VMEM((A_HEADS, QB, A_BAND), BF16),
            pltpu.VMEM((A_HEADS, QB, LANES), F32),
        ],
        compiler_params=pltpu.CompilerParams(
            dimension_semantics=("arbitrary", "arbitrary"), vmem_limit_bytes=VMEM_LIMIT),
        name="attn_a",
    )(qa, ka, va, bias)


def _rel_bias_tiles(rel_bias):
    pad = N_LEFT_CHUNKS * CHUNK
    width = pad + A_BAND
    length = width + QB
    n_edge = length - 1 - (2 * REL_CLIP + 1)
    assert n_edge % 2 == 0
    rb = rel_bias.astype(F32)
    h = rb.shape[0]
    f = jnp.concatenate([jnp.broadcast_to(rb[:, -1:], (h, n_edge // 2)), rb[:, ::-1],
                         jnp.broadcast_to(rb[:, :1], (h, n_edge // 2 + 1))], axis=1)
    f = jnp.roll(f, -(QB - 1), axis=1)
    toep = jnp.tile(f, (1, QB))[:, :QB * (length - 1)].reshape(h, QB, length - 1)[:, :, :width]

    cpb = QB // CHUNK
    r = np.arange(QB)
    c = np.arange(A_BAND)
    tiles = []
    for v in range(A_VARIANTS):
        c0 = v * cpb
        band_start = max(c0 - N_LEFT_CHUNKS, 0)
        delta = (c0 + r[:, None] // CHUNK) - (band_start + c[None, :] // CHUNK)
        ok = (delta >= 0) & (delta <= N_LEFT_CHUNKS)
        col0 = pad - (c0 - band_start) * CHUNK
        tiles.append(jnp.where(ok[None], toep[:, :, col0:col0 + A_BAND], NEG))
    return jnp.stack(tiles, axis=0)


def _ordered_to_float(u):
    bits = jnp.where(u < 0, u ^ jnp.int32(-2147483648), ~u)
    return pltpu.bitcast(bits, F32)


def _for_steps(n, body):
    def main(i, carry):
        body(i * 4, 4)
        return carry
    lax.fori_loop(0, n // 4, main, 0)
    done = (n // 4) * 4

    @pl.when((n & 2) != 0)
    def _pair():
        body(done, 2)

    @pl.when((n & 1) != 0)
    def _single():
        body(n - 1, 1)


def _bit_transpose(words):
    a = list(words)
    j, mask = KEY_BITS // 2, 0x0000FFFF
    while j:
        k = 0
        while k < KEY_BITS:
            t = (a[k] ^ lax.shift_right_logical(a[k + j], jnp.int32(j))) & mask
            a[k] = a[k] ^ t
            a[k + j] = a[k + j] ^ jnp.left_shift(t, jnp.int32(j))
            k = (k + j + 1) & ~j
        j >>= 1
        mask ^= (mask << j) & 0xFFFFFFFF
    return a


def _attn_b_kernel(q_ref, iq_ref, iw_ref, kb_ref, vb_ref, ik_ref, o_ref,
                   st_ref, plane_ref, sel_ref, s_ref, mrun_ref, acc_ref):
    qi = pl.program_id(1)
    n_tiles = qi + 1
    n_steps = (n_tiles + 1) // 2
    n_max = st_ref.shape[0] // KT
    rows = B_HEADS * QB
    int_min = jnp.int32(-2 ** 31)

    @pl.when(qi == 0)
    def _init_planes():
        plane_ref[...] = jnp.zeros(plane_ref.shape, jnp.int32)

    qs = q_ref[0].reshape(rows, LANES)
    iqs = iq_ref[0].reshape(rows, LANES)
    iw_t = iw_ref[0].T

    key_iota = lax.broadcasted_iota(jnp.int32, (KT, LANES), 0)
    q_lane = _lane_iota((1, LANES))
    key_limit = qi * QB + CHUNK + jnp.where(q_lane >= CHUNK, CHUNK, 0)

    def score_steps(t0, cnt):
        for g in range(cnt):
            off = pl.multiple_of((t0 + g) * KT, KT)
            lg = lax.dot_general(ik_ref[0, pl.ds(off, KT), :], iqs, NT_DIMS,
                                 preferred_element_type=F32)
            sc = None
            for h in range(IDX_HEADS):
                term = jnp.maximum(lg[:, h * QB:(h + 1) * QB], 0.0) * iw_t[h:h + 1, :]
                sc = term if sc is None else sc + term
            sc = jnp.where(key_iota + off < key_limit, sc + 0.0, -jnp.inf)
            st_ref[pl.ds(off, KT), :] = sc
            bits = pltpu.bitcast(sc, jnp.int32)
            key = bits ^ (jnp.right_shift(bits, 31) & 0x7FFFFFFF) ^ int_min
            planes = _bit_transpose([key[j * SUBLANES:(j + 1) * SUBLANES] for j in range(KEY_BITS)])
            row0 = pl.multiple_of((t0 + g) * SUBLANES, SUBLANES)
            for i in range(KEY_BITS):
                plane_ref[i, pl.ds(row0, SUBLANES), :] = planes[i]
    _for_steps(n_steps, score_steps)

    sel_ref[0:1, :] = jnp.full((1, LANES), -jnp.inf, F32)
    sel_ref[1:2, :] = jnp.full((1, LANES), -1.0, F32)

    def count(pred):
        def body(t, acc):
            off = pl.multiple_of(t * KT, KT)
            hit = jnp.where(pred(st_ref[pl.ds(off, KT), :], key_iota + off), 1.0, 0.0)
            return acc + hit[:QB] + hit[QB:]

        def body4(i, acc):
            for g in range(4):
                acc = body(i * 4 + g, acc)
            return acc
        acc = lax.fori_loop(0, n_steps // 4, body4, jnp.zeros((QB, LANES), F32))
        acc = lax.fori_loop((n_steps // 4) * 4, n_steps, body, acc)
        return jnp.sum(acc, axis=0, keepdims=True)

    k_sel = float(TOPK_MAX)

    @pl.when(n_tiles * QB - CHUNK > TOPK_MAX)
    def _search():
        step_rows = lax.broadcasted_iota(jnp.int32, (n_max * SUBLANES, LANES), 0)
        alive0 = jnp.where(step_rows < n_steps * SUBLANES, -1, 0)

        def bit_body(i, carry):
            alive, rem, key = carry
            ones = alive & plane_ref[i]
            n_ones = jnp.sum(lax.population_count(ones), axis=0, keepdims=True)
            take = n_ones >= rem
            alive = jnp.where(take, ones, alive ^ ones)
            rem = jnp.where(take, rem, rem - n_ones)
            key = key | jnp.where(take, jnp.left_shift(jnp.int32(1), KEY_BITS - 1 - i), 0)
            return alive, rem, key
        _, _, key = lax.fori_loop(0, KEY_BITS, bit_body,
                                  (alive0, jnp.full((1, LANES), TOPK_MAX, jnp.int32),
                                   jnp.zeros((1, LANES), jnp.int32)))
        key = key ^ int_min

        def publish(thr):
            sel_ref[0:1, :] = thr
            sel_ref[2:3, :] = count(lambda s, k: s > thr)
            sel_ref[3:4, :] = count(lambda s, k: s >= thr)
        publish(pltpu.bitcast(key ^ (jnp.right_shift(key, 31) & 0x7FFFFFFF), F32))

        holds = (sel_ref[2:3, :] < k_sel) & (sel_ref[3:4, :] >= k_sel)

        @pl.when(jnp.min(jnp.where(holds, 1.0, 0.0)) < 1.0)
        def _float_search():
            def bit_body(i, u):
                trial = u | jnp.left_shift(jnp.int32(1), 31 - i)
                cand = _ordered_to_float(trial)
                return jnp.where(count(lambda s, k: s >= cand) >= k_sel, trial, u)
            publish(_ordered_to_float(lax.fori_loop(0, 32, bit_body, jnp.zeros((1, LANES), jnp.int32))))

        thr = sel_ref[0:1, :]
        n_ge = sel_ref[3:4, :]
        need = k_sel - sel_ref[2:3, :]
        sel_ref[1:2, :] = jnp.full((1, LANES), 2.0 ** 30, F32)

        @pl.when(jnp.max(n_ge) > k_sel)
        def _ties():
            def idx_body(i, x):
                trial = x | jnp.left_shift(jnp.int32(1), 11 - i)
                cnt = count(lambda s, k: (s == thr) & (k < trial))
                return jnp.where(cnt < need, trial, x)
            x = lax.fori_loop(0, 12, idx_body, jnp.zeros((1, LANES), jnp.int32))
            sel_ref[1:2, :] = x.astype(F32)

    thr = sel_ref[0:1, :]
    cut = sel_ref[1:2, :].astype(jnp.int32)

    half_iota = key_iota[:QB]
    mrun_ref[...] = jnp.full((rows, LANES), NEG, F32)

    def pass1_steps(t0, cnt):
        best = None
        for g in range(cnt):
            off = pl.multiple_of((t0 + g) * KT, KT)
            s = lax.dot_general(qs, kb_ref[0, pl.ds(off, KT), :], NT_DIMS,
                                preferred_element_type=F32)
            halves = []
            for half in range(KT // QB):
                sc = st_ref[pl.ds(off + half * QB, QB), :]
                tie = jnp.where(half_iota + (off + half * QB) <= cut, 0.0, NEG)
                halves.append(jnp.where(sc > thr, 0.0, jnp.where(sc == thr, tie, NEG)).T)
            mask = jnp.concatenate(halves, axis=1)
            s = (s.reshape(B_HEADS, QB, KT) + mask[None]).reshape(rows, KT)
            s_ref[t0 + g] = s
            top = jnp.maximum(s[:, :LANES], s[:, LANES:])
            best = top if best is None else jnp.maximum(best, top)
        mrun_ref[...] = jnp.maximum(mrun_ref[...], best)
    _for_steps(n_steps, pass1_steps)

    m_row = jnp.max(mrun_ref[...], axis=1, keepdims=True)
    mrun_ref[...] = jnp.broadcast_to(m_row, (rows, LANES))
    acc_ref[...] = jnp.zeros((rows, LANES), F32)

    def pass2_steps(t0, cnt):
        m_b = mrun_ref[...]
        total = None
        for g in range(cnt):
            off = pl.multiple_of((t0 + g) * KT, KT)
            s = s_ref[t0 + g]
            p = jnp.concatenate([jnp.exp(s[:, :LANES] - m_b), jnp.exp(s[:, LANES:] - m_b)], axis=1)
            pv = jnp.dot(p.astype(BF16), vb_ref[0, pl.ds(off, KT), :], preferred_element_type=F32)
            total = pv if total is None else total + pv
        acc_ref[...] += total
    _for_steps(n_steps, pass2_steps)

    acc = acc_ref[...]
    out = acc / pltpu.roll(acc, B_HEAD_DIM, 1)
    lane = _lane_iota((QB, LANES))
    for g in range(B_WIDTH // LANES):
        lo = out[(2 * g) * QB:(2 * g + 1) * QB]
        hi = pltpu.roll(out[(2 * g + 1) * QB:(2 * g + 2) * QB], B_HEAD_DIM, 1)
        o_ref[0, :, g * LANES:(g + 1) * LANES] = jnp.where(lane < B_HEAD_DIM, lo, hi)


def _attn_b_call(qb, iq, iw, kb, vb, ik, batch, seq):
    n_q = seq // QB
    rows = B_HEADS * QB
    blk = lambda b, i: (b * n_q + i, 0, 0, 0)
    res = lambda b, i: (b, 0, 0)
    return pl.pallas_call(
        _attn_b_kernel,
        grid=(batch, n_q),
        in_specs=[
            pl.BlockSpec((1, B_HEADS, QB, LANES), blk),
            pl.BlockSpec((1, IDX_HEADS, QB, LANES), blk),
            pl.BlockSpec((1, QB, LANES), lambda b, i: (b, i, 0)),
            pl.BlockSpec((1, seq, LANES), res),
            pl.BlockSpec((1, seq, LANES), res),
            pl.BlockSpec((1, seq, LANES), res),
        ],
        out_specs=pl.BlockSpec((1, QB, B_WIDTH), lambda b, i: (b, i, 0)),
        out_shape=jax.ShapeDtypeStruct((batch, seq, B_WIDTH), F32),
        scratch_shapes=[
            pltpu.VMEM((seq, LANES), F32),
            pltpu.VMEM((KEY_BITS, seq // KT * SUBLANES, LANES), jnp.int32),
            pltpu.VMEM((SUBLANES, LANES), F32),
            pltpu.VMEM((seq // KT, rows, KT), F32),
            pltpu.VMEM((rows, LANES), F32),
            pltpu.VMEM((rows, LANES), F32),
        ],
        compiler_params=pltpu.CompilerParams(
            dimension_semantics=("arbitrary", "arbitrary"), vmem_limit_bytes=VMEM_LIMIT),
        name="attn_b",
    )(qb, iq, iw, kb, vb, ik)


def _sigmoid(z):
    return 1.0 / (1.0 + jnp.exp(-z))


def _out_kernel(x_ref, g_ref, wg_ref, bm_ref, ya_ref, yb_ref, wa_ref, wb_ref, wo_ref, fg_ref, o_ref):
    x = x_ref[...]
    xn = _rms(x, g_ref[...]).astype(BF16)

    def branch(y_ref, col, w_ref, bias_row):
        gate = jnp.dot(xn, wg_ref[:, col:col + A_WIDTH], preferred_element_type=F32)
        y = (y_ref[...] * (gate * _sigmoid(gate))).astype(BF16)
        pr = jnp.dot(y, w_ref[...], preferred_element_type=F32)
        zcol = 2 * A_WIDTH + bias_row * D_MODEL
        z = jnp.dot(xn, wg_ref[:, zcol:zcol + D_MODEL], preferred_element_type=F32)
        return _sigmoid(z + bm_ref[bias_row:bias_row + 1, :]) * pr

    merged = branch(ya_ref, 0, wa_ref, 0) + branch(yb_ref, A_WIDTH, wb_ref, 1)
    h = x + jnp.dot(merged.astype(BF16), wo_ref[...], preferred_element_type=F32)
    o_ref[...] = _rms(h, fg_ref[...])


def _out_call(x2, gain, w2, bm, ya, yb, wa, wb, wo, fgain):
    m = x2.shape[0]
    tm = OUT_TM
    row = lambda i: (i, 0)
    full = lambda a: pl.BlockSpec(a.shape, lambda i: (0, 0))
    return pl.pallas_call(
        _out_kernel,
        grid=(m // tm,),
        in_specs=[
            pl.BlockSpec((tm, D_MODEL), row), full(gain), full(w2), full(bm),
            pl.BlockSpec((tm, A_WIDTH), row), pl.BlockSpec((tm, B_WIDTH), row),
            full(wa), full(wb), full(wo), full(fgain),
        ],
        out_specs=pl.BlockSpec((tm, D_MODEL), row),
        out_shape=jax.ShapeDtypeStruct((m, D_MODEL), F32),
        compiler_params=pltpu.CompilerParams(
            dimension_semantics=("arbitrary",), vmem_limit_bytes=VMEM_LIMIT),
        name="out",
    )(x2, gain, w2, bm, ya, yb, wa, wb, wo, fgain)


def _rope_tables(seq):
    pos = jnp.arange(seq, dtype=F32)[:, None]

    def pattern(head_dim, rot_dim):
        half = rot_dim // 2
        inv = ROPE_THETA ** (-jnp.arange(half, dtype=F32) / half)
        ang = pos * inv[None, :]
        cos, sin = jnp.cos(ang), jnp.sin(ang)
        ones = jnp.ones((seq, head_dim - rot_dim), F32)
        c = jnp.concatenate([cos, cos, ones], axis=1)
        s = jnp.concatenate([-sin, sin, 0.0 * ones], axis=1)
        reps = LANES // head_dim
        return jnp.tile(c, (1, reps)), jnp.tile(s, (1, reps))

    cq, sq = pattern(B_HEAD_DIM, ROT_DIM_B)
    ci, si = pattern(IDX_DIM, ROT_DIM_IDX)
    return jnp.concatenate([cq, sq, ci, si], axis=1)


def kernel(x, norm_gain, w_in, b_merge, rel_bias, w_branch_a, w_branch_b, w_out, final_norm_gain):
    batch, seq, d = x.shape
    assert d == D_MODEL and seq % PROJ_TM == 0 and norm_gain.shape[0] == 1
    m = batch * seq
    x2 = x.reshape(m, d)
    offsets = [int(o) for o in np.cumsum(SPLIT_SIZES)[:-1]]
    (w_qa, w_ka, w_va, w_ga, w_qb, w_kb, w_vb, w_gb,
     w_iq, w_ik, w_iw, w_za, w_zb) = jnp.split(w_in[0], offsets, axis=1)
    pad = jnp.zeros((d, LANES - IDX_DIM - IDX_HEADS), w_in.dtype)
    w1 = jnp.concatenate([w_qa, w_ka, w_va, w_qb, w_iq, w_kb, w_vb, w_ik, w_iw, pad], axis=1).astype(BF16)
    w2 = jnp.concatenate([w_ga, w_gb, w_za, w_zb], axis=1).astype(BF16)

    qa, ka, va, qb, iq, kb, vb, ik, iw = _proj_call(x2, norm_gain, w1, _rope_tables(seq), batch, seq)

    to3 = lambda a: a.reshape(batch, seq, a.shape[-1])
    ya = _attn_a_call(qa, to3(ka), to3(va), _rel_bias_tiles(rel_bias[0]))
    yb = _attn_b_call(qb, iq, to3(iw), to3(kb), to3(vb), to3(ik), batch, seq)

    out = _out_call(x2, norm_gain, w2, b_merge[0], ya.reshape(m, A_WIDTH), yb.reshape(m, B_WIDTH),
                    w_branch_a[0].astype(BF16), w_branch_b[0].astype(BF16), w_out[0].astype(BF16),
                    final_norm_gain.reshape(1, d))
    return out.reshape(batch, seq, d)
```

```python
import jax
import jax.numpy as jnp
import numpy as np
from jax import lax
from jax.experimental import pallas as pl
from jax.experimental.pallas import tpu as pltpu

D_MODEL = 1024
CHUNK = 64
N_LEFT_CHUNKS = 8
A_HEADS = 8
A_HEAD_DIM = 64
A_WIDTH = A_HEADS * A_HEAD_DIM
REL_CLIP = 256
B_HEADS = 8
B_HEAD_DIM = 64
B_WIDTH = B_HEADS * B_HEAD_DIM
IDX_HEADS = 8
IDX_DIM = 32
TOPK_MAX = 256
ROPE_THETA = 500000.0
ROT_DIM_B = B_HEAD_DIM // 4
ROT_DIM_IDX = IDX_DIM // 4
EPS = 1e-6
NEG = -1e30

SPLIT_SIZES = (
    A_WIDTH, A_WIDTH, A_WIDTH, A_WIDTH,
    B_WIDTH, B_HEAD_DIM, B_HEAD_DIM, B_WIDTH,
    IDX_HEADS * IDX_DIM, IDX_DIM, IDX_HEADS,
    D_MODEL, D_MODEL,
)

LANES = 128
SUBLANES = 8
QB = 128
KT = 2 * QB
KEY_BITS = 32
assert KT == KEY_BITS * SUBLANES
A_BAND = QB + N_LEFT_CHUNKS * CHUNK
A_VARIANTS = N_LEFT_CHUNKS * CHUNK // QB + 1
PROJ_TM = 1024
OUT_TM = 512
VMEM_LIMIT = 48 * 1024 * 1024

BF16 = jnp.bfloat16
F32 = jnp.float32
NT_DIMS = (((1,), (1,)), ((), ()))


def _rms(x, g):
    ms = jnp.mean(x * x, axis=-1, keepdims=True)
    return (x * lax.rsqrt(ms + EPS)) * g


def _lane_iota(shape):
    return lax.broadcasted_iota(jnp.int32, shape, len(shape) - 1)


def _rope(xg, c, s, half):
    lane = _lane_iota(xg.shape)
    first = (lane % (2 * half)) < half
    partner = jnp.where(first, pltpu.roll(xg, LANES - half, 1), pltpu.roll(xg, half, 1))
    return xg * c + partner * s


def _proj_kernel(x_ref, g_ref, w_ref, tab_ref,
                 qa_ref, ka_ref, va_ref, qb_ref, iq_ref, kb_ref, vb_ref, ik_ref, iw_ref):
    xn = _rms(x_ref[...], g_ref[...])
    p = jnp.dot(xn.astype(BF16), w_ref[...], preferred_element_type=F32)
    tm = p.shape[0]
    nblk = tm // QB
    lane = _lane_iota((tm, LANES))

    a_scale = A_HEAD_DIM ** -0.5
    b_scale = B_HEAD_DIM ** -0.5
    for g in range(A_WIDTH // LANES):
        xg = p[:, g * LANES:(g + 1) * LANES] * a_scale
        for r in range(LANES // A_HEAD_DIM):
            own = (lane // A_HEAD_DIM) == r
            piece = jnp.where(own, xg, 0.0).astype(BF16)
            for j in range(nblk):
                qa_ref[j, g * (LANES // A_HEAD_DIM) + r] = piece[j * QB:(j + 1) * QB]
    ka_ref[...] = p[:, A_WIDTH:2 * A_WIDTH].astype(BF16)
    va_ref[...] = p[:, 2 * A_WIDTH:3 * A_WIDTH].astype(BF16)

    cq = tab_ref[:, 0:LANES]
    sq = tab_ref[:, LANES:2 * LANES]
    ci = tab_ref[:, 2 * LANES:3 * LANES]
    si = tab_ref[:, 3 * LANES:4 * LANES]

    base = 3 * A_WIDTH
    for g in range(B_WIDTH // LANES):
        xg = _rope(p[:, base + g * LANES: base + (g + 1) * LANES], cq, sq, ROT_DIM_B // 2) * b_scale
        per_group = LANES // B_HEAD_DIM
        for r in range(per_group):
            h = g * per_group + r
            piece = xg if r == 0 else pltpu.roll(xg, LANES - r * B_HEAD_DIM, 1)
            piece = jnp.where(lane < B_HEAD_DIM, piece, 0.0).astype(BF16)
            for j in range(nblk):
                qb_ref[j, h] = piece[j * QB:(j + 1) * QB]

    base = 3 * A_WIDTH + B_WIDTH
    for g in range(IDX_HEADS * IDX_DIM // LANES):
        xg = _rope(p[:, base + g * LANES: base + (g + 1) * LANES], ci, si, ROT_DIM_IDX // 2)
        per_group = LANES // IDX_DIM
        for r in range(per_group):
            h = g * per_group + r
            piece = xg if r == 0 else pltpu.roll(xg, LANES - r * IDX_DIM, 1)
            piece = jnp.where(lane < IDX_DIM, piece, 0.0).astype(BF16)
            for j in range(nblk):
                iq_ref[j, h] = piece[j * QB:(j + 1) * QB]

    base = 3 * A_WIDTH + B_WIDTH + IDX_HEADS * IDX_DIM
    is_k = lane < B_HEAD_DIM
    kv = _rope(p[:, base:base + LANES], jnp.where(is_k, cq, 1.0), jnp.where(is_k, sq, 0.0),
               ROT_DIM_B // 2)
    kb_ref[...] = jnp.where(is_k, kv, 0.0).astype(BF16)
    vb_ref[...] = jnp.where(is_k, pltpu.roll(kv, LANES - B_HEAD_DIM, 1), 1.0).astype(BF16)

    base = base + LANES
    is_ik = lane < IDX_DIM
    kw = _rope(p[:, base:base + LANES], jnp.where(is_ik, ci, 1.0), jnp.where(is_ik, si, 0.0),
               ROT_DIM_IDX // 2)
    ik_ref[...] = jnp.where(is_ik, kw, 0.0).astype(BF16)
    iw_scale = IDX_HEADS ** -0.5 * IDX_DIM ** -0.5
    iw_ref[...] = pltpu.roll(kw, LANES - IDX_DIM, 1) * iw_scale


def _proj_call(x2, gain, w1, tab, batch, seq):
    m = x2.shape[0]
    tm = PROJ_TM
    n_s = seq // tm
    nblk = tm // QB
    row = lambda s, b: (b * n_s + s, 0)
    blk4 = lambda s, b: (b * n_s + s, 0, 0, 0)
    wide = lambda width, dtype: jax.ShapeDtypeStruct((m, width), dtype)
    stacked = jax.ShapeDtypeStruct((m // QB, B_HEADS, QB, LANES), BF16)
    return pl.pallas_call(
        _proj_kernel,
        grid=(n_s, batch),
        in_specs=[
            pl.BlockSpec((tm, D_MODEL), row),
            pl.BlockSpec((1, D_MODEL), lambda s, b: (0, 0)),
            pl.BlockSpec(w1.shape, lambda s, b: (0, 0)),
            pl.BlockSpec((tm, 4 * LANES), lambda s, b: (s, 0)),
        ],
        out_specs=[
            pl.BlockSpec((nblk, A_HEADS, QB, LANES), blk4),
            pl.BlockSpec((tm, A_WIDTH), row),
            pl.BlockSpec((tm, A_WIDTH), row),
            pl.BlockSpec((nblk, B_HEADS, QB, LANES), blk4),
            pl.BlockSpec((nblk, IDX_HEADS, QB, LANES), blk4),
            pl.BlockSpec((tm, LANES), row),
            pl.BlockSpec((tm, LANES), row),
            pl.BlockSpec((tm, LANES), row),
            pl.BlockSpec((tm, LANES), row),
        ],
        out_shape=[
            stacked, wide(A_WIDTH, BF16), wide(A_WIDTH, BF16),
            stacked, stacked,
            wide(LANES, BF16), wide(LANES, BF16), wide(LANES, BF16), wide(LANES, F32),
        ],
        compiler_params=pltpu.CompilerParams(
            dimension_semantics=("arbitrary", "arbitrary"), vmem_limit_bytes=VMEM_LIMIT),
        name="proj",
    )(x2, gain, w1, tab)


def _attn_a_kernel(q_ref, k_ref, v_ref, bias_ref, o_ref, s_ref, p_ref, rinv_ref):
    qi = pl.program_id(1)
    start = pl.multiple_of(jnp.maximum(qi * QB - N_LEFT_CHUNKS * CHUNK, 0), QB)
    lane = _lane_iota((QB, LANES))
    per_group = LANES // A_HEAD_DIM
    group_cols = lambda h: slice((h // per_group) * LANES, (h // per_group + 1) * LANES)

    for h in range(A_HEADS):
        kg = k_ref[0, pl.ds(start, A_BAND), group_cols(h)]
        s_ref[h] = lax.dot_general(q_ref[0, h], kg, NT_DIMS, preferred_element_type=F32) + bias_ref[0, h]
    for h in range(A_HEADS):
        s = s_ref[h]
        e = jnp.exp(s - jnp.max(s, axis=-1, keepdims=True))
        p_ref[h] = e.astype(BF16)
        rinv_ref[h] = jnp.broadcast_to(1.0 / jnp.sum(e, axis=-1, keepdims=True), (QB, LANES))
    for g in range(A_WIDTH // LANES):
        out = None
        for r in range(per_group):
            h = g * per_group + r
            vg = v_ref[0, pl.ds(start, A_BAND), group_cols(h)]
            o = jnp.dot(p_ref[h], vg, preferred_element_type=F32) * rinv_ref[h]
            out = o if out is None else jnp.where((lane // A_HEAD_DIM) == r, o, out)
        o_ref[0, :, g * LANES:(g + 1) * LANES] = out


def _attn_a_call(qa, ka, va, bias):
    batch, seq, _ = ka.shape
    n_q = seq // QB
    return pl.pallas_call(
        _attn_a_kernel,
        grid=(batch, n_q),
        in_specs=[
            pl.BlockSpec((1, A_HEADS, QB, LANES), lambda b, i: (b * n_q + i, 0, 0, 0)),
            pl.BlockSpec((1, seq, A_WIDTH), lambda b, i: (b, 0, 0)),
            pl.BlockSpec((1, seq, A_WIDTH), lambda b, i: (b, 0, 0)),
            pl.BlockSpec((1, A_HEADS, QB, A_BAND),
                         lambda b, i: (jnp.minimum(i, A_VARIANTS - 1), 0, 0, 0)),
        ],
        out_specs=pl.BlockSpec((1, QB, A_WIDTH), lambda b, i: (b, i, 0)),
        out_shape=jax.ShapeDtypeStruct((batch, seq, A_WIDTH), F32),
        scratch_shapes=[
            pltpu.VMEM((A_HEADS, QB, A_BAND), F32),
            pltpu.VMEM((A_HEADS, QB, A_BAND), BF16),
            pltpu.VMEM((A_HEADS, QB, LANES), F32),
        ],
        compiler_params=pltpu.CompilerParams(
            dimension_semantics=("arbitrary", "arbitrary"), vmem_limit_bytes=VMEM_LIMIT),
        name="attn_a",
    )(qa, ka, va, bias)


def _rel_bias_tiles(rel_bias):
    pad = N_LEFT_CHUNKS * CHUNK
    width = pad + A_BAND
    length = width + QB
    n_edge = length - 1 - (2 * REL_CLIP + 1)
    assert n_edge % 2 == 0
    rb = rel_bias.astype(F32)
    h = rb.shape[0]
    f = jnp.concatenate([jnp.broadcast_to(rb[:, -1:], (h, n_edge // 2)), rb[:, ::-1],
                         jnp.broadcast_to(rb[:, :1], (h, n_edge // 2 + 1))], axis=1)
    f = jnp.roll(f, -(QB - 1), axis=1)
    toep = jnp.tile(f, (1, QB))[:, :QB * (length - 1)].reshape(h, QB, length - 1)[:, :, :width]

    cpb = QB // CHUNK
    r = np.arange(QB)
    c = np.arange(A_BAND)
    tiles = []
    for v in range(A_VARIANTS):
        c0 = v * cpb
        band_start = max(c0 - N_LEFT_CHUNKS, 0)
        delta = (c0 + r[:, None] // CHUNK) - (band_start + c[None, :] // CHUNK)
        ok = (delta >= 0) & (delta <= N_LEFT_CHUNKS)
        col0 = pad - (c0 - band_start) * CHUNK
        tiles.append(jnp.where(ok[None], toep[:, :, col0:col0 + A_BAND], NEG))
    return jnp.stack(tiles, axis=0)


def _ordered_to_float(u):
    bits = jnp.where(u < 0, u ^ jnp.int32(-2147483648), ~u)
    return pltpu.bitcast(bits, F32)


def _for_steps(n, body):
    def main(i, carry):
        body(i * 4, 4)
        return carry
    lax.fori_loop(0, n // 4, main, 0)
    done = (n // 4) * 4

    @pl.when((n & 2) != 0)
    def _pair():
        body(done, 2)

    @pl.when((n & 1) != 0)
    def _single():
        body(n - 1, 1)


def _bit_transpose(words):
    a = list(words)
    j, mask = KEY_BITS // 2, 0x0000FFFF
    while j:
        k = 0
        while k < KEY_BITS:
            t = (a[k] ^ lax.shift_right_logical(a[k + j], jnp.int32(j))) & mask
            a[k] = a[k] ^ t
            a[k + j] = a[k + j] ^ jnp.left_shift(t, jnp.int32(j))
            k = (k + j + 1) & ~j
        j >>= 1
        mask ^= (mask << j) & 0xFFFFFFFF
    return a


def _attn_b_kernel(q_ref, iq_ref, iw_ref, kb_ref, vb_ref, ik_ref, o_ref,
                   st_ref, plane_ref, sel_ref, s_ref, mrun_ref, acc_ref):
    qi = pl.program_id(1)
    n_tiles = qi + 1
    n_steps = (n_tiles + 1) // 2
    n_max = st_ref.shape[0] // KT
    rows = B_HEADS * QB
    int_min = jnp.int32(-2 ** 31)

    @pl.when(qi == 0)
    def _init_planes():
        plane_ref[...] = jnp.zeros(plane_ref.shape, jnp.int32)

    qs = q_ref[0].reshape(rows, LANES)
    iqs = iq_ref[0].reshape(rows, LANES)
    iw_t = iw_ref[0].T

    key_iota = lax.broadcasted_iota(jnp.int32, (KT, LANES), 0)
    q_lane = _lane_iota((1, LANES))
    key_limit = qi * QB + CHUNK + jnp.where(q_lane >= CHUNK, CHUNK, 0)

    def score_steps(t0, cnt):
        for g in range(cnt):
            off = pl.multiple_of((t0 + g) * KT, KT)
            lg = lax.dot_general(ik_ref[0, pl.ds(off, KT), :], iqs, NT_DIMS,
                                 preferred_element_type=F32)
            sc = None
            for h in range(IDX_HEADS):
                term = jnp.maximum(lg[:, h * QB:(h + 1) * QB], 0.0) * iw_t[h:h + 1, :]
                sc = term if sc is None else sc + term
            sc = jnp.where(key_iota + off < key_limit, sc + 0.0, -jnp.inf)
            st_ref[pl.ds(off, KT), :] = sc
            bits = pltpu.bitcast(sc, jnp.int32)
            key = bits ^ (jnp.right_shift(bits, 31) & 0x7FFFFFFF) ^ int_min
            planes = _bit_transpose([key[j * SUBLANES:(j + 1) * SUBLANES] for j in range(KEY_BITS)])
            row0 = pl.multiple_of((t0 + g) * SUBLANES, SUBLANES)
            for i in range(KEY_BITS):
                plane_ref[i, pl.ds(row0, SUBLANES), :] = planes[i]
    _for_steps(n_steps, score_steps)

    sel_ref[0:1, :] = jnp.full((1, LANES), -jnp.inf, F32)
    sel_ref[1:2, :] = jnp.full((1, LANES), -1.0, F32)

    def count(pred):
        def body(t, acc):
            off = pl.multiple_of(t * KT, KT)
            hit = jnp.where(pred(st_ref[pl.ds(off, KT), :], key_iota + off), 1.0, 0.0)
            return acc + hit[:QB] + hit[QB:]

        def body4(i, acc):
            for g in range(4):
                acc = body(i * 4 + g, acc)
            return acc
        acc = lax.fori_loop(0, n_steps // 4, body4, jnp.zeros((QB, LANES), F32))
        acc = lax.fori_loop((n_steps // 4) * 4, n_steps, body, acc)
        return jnp.sum(acc, axis=0, keepdims=True)

    k_sel = float(TOPK_MAX)

    @pl.when(n_tiles * QB - CHUNK > TOPK_MAX)
    def _search():
        step_rows = lax.broadcasted_iota(jnp.int32, (n_max * SUBLANES, LANES), 0)
        alive0 = jnp.where(step_rows < n_steps * SUBLANES, -1, 0)

        def bit_body(i, carry):
            alive, rem, key = carry
            ones = alive & plane_ref[i]
            n_ones = jnp.sum(lax.population_count(ones), axis=0, keepdims=True)
            take = n_ones >= rem
            alive = jnp.where(take, ones, alive ^ ones)
            rem = jnp.where(take, rem, rem - n_ones)
            key = key | jnp.where(take, jnp.left_shift(jnp.int32(1), KEY_BITS - 1 - i), 0)
            return alive, rem, key
        alive, rem, key = lax.fori_loop(0, KEY_BITS, bit_body,
                                  (alive0, jnp.full((1, LANES), TOPK_MAX, jnp.int32),
                                   jnp.zeros((1, LANES), jnp.int32)))
        key = key ^ int_min

        def publish(thr):
            sel_ref[0:1, :] = thr
            sel_ref[2:3, :] = count(lambda s, k: s > thr)
            sel_ref[3:4, :] = count(lambda s, k: s >= thr)
        publish(pltpu.bitcast(key ^ (jnp.right_shift(key, 31) & 0x7FFFFFFF), F32))

        holds = (sel_ref[2:3, :] < k_sel) & (sel_ref[3:4, :] >= k_sel)

        @pl.when(jnp.min(jnp.where(holds, 1.0, 0.0)) < 1.0)
        def _float_search():
            def bit_body(i, u):
                trial = u | jnp.left_shift(jnp.int32(1), 31 - i)
                cand = _ordered_to_float(trial)
                return jnp.where(count(lambda s, k: s >= cand) >= k_sel, trial, u)
            publish(_ordered_to_float(lax.fori_loop(0, 32, bit_body, jnp.zeros((1, LANES), jnp.int32))))

        thr = sel_ref[0:1, :]
        n_ge = sel_ref[3:4, :]
        need = k_sel - sel_ref[2:3, :]
        sel_ref[1:2, :] = jnp.full((1, LANES), 2.0 ** 30, F32)

        @pl.when(jnp.max(n_ge) > k_sel)
        def _ties():
            word_key = (step_rows // SUBLANES) * KT + step_rows % SUBLANES
            ones = jnp.full(step_rows.shape, -1, jnp.int32)

            def plane_body(i, x):
                trial = x | jnp.left_shift(jnp.int32(1), 11 - i)
                n_low = jnp.clip(jnp.right_shift(trial - word_key + (SUBLANES - 1), 3), 0, KEY_BITS)
                rest = lax.shift_right_logical(ones, jnp.minimum(n_low, KEY_BITS - 1))
                below = jnp.where(n_low >= KEY_BITS, ones, ~rest)
                cnt = jnp.sum(lax.population_count(alive & below), axis=0, keepdims=True)
                return jnp.where(cnt < rem, trial, x)
            x = lax.fori_loop(0, 12, plane_body, jnp.zeros((1, LANES), jnp.int32))
            sel_ref[1:2, :] = x.astype(F32)

            n_cut = count(lambda s, k: (s == thr) & (k <= x))

            @pl.when(jnp.min(jnp.where(n_cut == need, 1.0, 0.0)) < 1.0)
            def _float_ties():
                def idx_body(i, x):
                    trial = x | jnp.left_shift(jnp.int32(1), 11 - i)
                    cnt = count(lambda s, k: (s == thr) & (k < trial))
                    return jnp.where(cnt < need, trial, x)
                x = lax.fori_loop(0, 12, idx_body, jnp.zeros((1, LANES), jnp.int32))
                sel_ref[1:2, :] = x.astype(F32)

    thr = sel_ref[0:1, :]
    cut = sel_ref[1:2, :].astype(jnp.int32)

    half_iota = key_iota[:QB]
    mrun_ref[...] = jnp.full((rows, LANES), NEG, F32)

    def pass1_steps(t0, cnt):
        best = None
        for g in range(cnt):
            off = pl.multiple_of((t0 + g) * KT, KT)
            s = lax.dot_general(qs, kb_ref[0, pl.ds(off, KT), :], NT_DIMS,
                                preferred_element_type=F32)
            halves = []
            for half in range(KT // QB):
                sc = st_ref[pl.ds(off + half * QB, QB), :]
                tie = jnp.where(half_iota + (off + half * QB) <= cut, 0.0, NEG)
                halves.append(jnp.where(sc > thr, 0.0, jnp.where(sc == thr, tie, NEG)).T)
            mask = jnp.concatenate(halves, axis=1)
            s = (s.reshape(B_HEADS, QB, KT) + mask[None]).reshape(rows, KT)
            s_ref[t0 + g] = s
            top = jnp.maximum(s[:, :LANES], s[:, LANES:])
            best = top if best is None else jnp.maximum(best, top)
        mrun_ref[...] = jnp.maximum(mrun_ref[...], best)
    _for_steps(n_steps, pass1_steps)

    m_row = jnp.max(mrun_ref[...], axis=1, keepdims=True)
    mrun_ref[...] = jnp.broadcast_to(m_row, (rows, LANES))
    acc_ref[...] = jnp.zeros((rows, LANES), F32)

    def pass2_steps(t0, cnt):
        m_b = mrun_ref[...]
        total = None
        for g in range(cnt):
            off = pl.multiple_of((t0 + g) * KT, KT)
            s = s_ref[t0 + g]
            p = jnp.concatenate([jnp.exp(s[:, :LANES] - m_b), jnp.exp(s[:, LANES:] - m_b)], axis=1)
            pv = jnp.dot(p.astype(BF16), vb_ref[0, pl.ds(off, KT), :], preferred_element_type=F32)
            total = pv if total is None else total + pv
        acc_ref[...] += total
    _for_steps(n_steps, pass2_steps)

    acc = acc_ref[...]
    out = acc / pltpu.roll(acc, B_HEAD_DIM, 1)
    lane = _lane_iota((QB, LANES))
    for g in range(B_WIDTH // LANES):
        lo = out[(2 * g) * QB:(2 * g + 1) * QB]
        hi = pltpu.roll(out[(2 * g + 1) * QB:(2 * g + 2) * QB], B_HEAD_DIM, 1)
        o_ref[0, :, g * LANES:(g + 1) * LANES] = jnp.where(lane < B_HEAD_DIM, lo, hi)


def _attn_b_call(qb, iq, iw, kb, vb, ik, batch, seq):
    n_q = seq // QB
    rows = B_HEADS * QB
    blk = lambda b, i: (b * n_q + i, 0, 0, 0)
    res = lambda b, i: (b, 0, 0)
    return pl.pallas_call(
        _attn_b_kernel,
        grid=(batch, n_q),
        in_specs=[
            pl.BlockSpec((1, B_HEADS, QB, LANES), blk),
            pl.BlockSpec((1, IDX_HEADS, QB, LANES), blk),
            pl.BlockSpec((1, QB, LANES), lambda b, i: (b, i, 0)),
            pl.BlockSpec((1, seq, LANES), res),
            pl.BlockSpec((1, seq, LANES), res),
            pl.BlockSpec((1, seq, LANES), res),
        ],
        out_specs=pl.BlockSpec((1, QB, B_WIDTH), lambda b, i: (b, i, 0)),
        out_shape=jax.ShapeDtypeStruct((batch, seq, B_WIDTH), F32),
        scratch_shapes=[
            pltpu.VMEM((seq, LANES), F32),
            pltpu.VMEM((KEY_BITS, seq // KT * SUBLANES, LANES), jnp.int32),
            pltpu.VMEM((SUBLANES, LANES), F32),
            pltpu.VMEM((seq // KT, rows, KT), F32),
            pltpu.VMEM((rows, LANES), F32),
            pltpu.VMEM((rows, LANES), F32),
        ],
        compiler_params=pltpu.CompilerParams(
            dimension_semantics=("arbitrary", "arbitrary"), vmem_limit_bytes=VMEM_LIMIT),
        name="attn_b",
    )(qb, iq, iw, kb, vb, ik)


def _sigmoid(z):
    return 1.0 / (1.0 + jnp.exp(-z))


def _out_kernel(x_ref, g_ref, wg_ref, bm_ref, ya_ref, yb_ref, wa_ref, wb_ref, wo_ref, fg_ref, o_ref):
    x = x_ref[...]
    xn = _rms(x, g_ref[...]).astype(BF16)

    def branch(y_ref, col, w_ref, bias_row):
        gate = jnp.dot(xn, wg_ref[:, col:col + A_WIDTH], preferred_element_type=F32)
        y = (y_ref[...] * (gate * _sigmoid(gate))).astype(BF16)
        pr = jnp.dot(y, w_ref[...], preferred_element_type=F32)
        zcol = 2 * A_WIDTH + bias_row * D_MODEL
        z = jnp.dot(xn, wg_ref[:, zcol:zcol + D_MODEL], preferred_element_type=F32)
        return _sigmoid(z + bm_ref[bias_row:bias_row + 1, :]) * pr

    merged = branch(ya_ref, 0, wa_ref, 0) + branch(yb_ref, A_WIDTH, wb_ref, 1)
    h = x + jnp.dot(merged.astype(BF16), wo_ref[...], preferred_element_type=F32)
    o_ref[...] = _rms(h, fg_ref[...])


def _out_call(x2, gain, w2, bm, ya, yb, wa, wb, wo, fgain):
    m = x2.shape[0]
    tm = OUT_TM
    row = lambda i: (i, 0)
    full = lambda a: pl.BlockSpec(a.shape, lambda i: (0, 0))
    return pl.pallas_call(
        _out_kernel,
        grid=(m // tm,),
        in_specs=[
            pl.BlockSpec((tm, D_MODEL), row), full(gain), full(w2), full(bm),
            pl.BlockSpec((tm, A_WIDTH), row), pl.BlockSpec((tm, B_WIDTH), row),
            full(wa), full(wb), full(wo), full(fgain),
        ],
        out_specs=pl.BlockSpec((tm, D_MODEL), row),
        out_shape=jax.ShapeDtypeStruct((m, D_MODEL), F32),
        compiler_params=pltpu.CompilerParams(
            dimension_semantics=("arbitrary",), vmem_limit_bytes=VMEM_LIMIT),
        name="out",
    )(x2, gain, w2, bm, ya, yb, wa, wb, wo, fgain)


def _rope_tables(seq):
    pos = jnp.arange(seq, dtype=F32)[:, None]

    def pattern(head_dim, rot_dim):
        half = rot_dim // 2
        inv = ROPE_THETA ** (-jnp.arange(half, dtype=F32) / half)
        ang = pos * inv[None, :]
        cos, sin = jnp.cos(ang), jnp.sin(ang)
        ones = jnp.ones((seq, head_dim - rot_dim), F32)
        c = jnp.concatenate([cos, cos, ones], axis=1)
        s = jnp.concatenate([-sin, sin, 0.0 * ones], axis=1)
        reps = LANES // head_dim
        return jnp.tile(c, (1, reps)), jnp.tile(s, (1, reps))

    cq, sq = pattern(B_HEAD_DIM, ROT_DIM_B)
    ci, si = pattern(IDX_DIM, ROT_DIM_IDX)
    return jnp.concatenate([cq, sq, ci, si], axis=1)


def kernel(x, norm_gain, w_in, b_merge, rel_bias, w_branch_a, w_branch_b, w_out, final_norm_gain):
    batch, seq, d = x.shape
    assert d == D_MODEL and seq % PROJ_TM == 0 and norm_gain.shape[0] == 1
    m = batch * seq
    x2 = x.reshape(m, d)
    offsets = [int(o) for o in np.cumsum(SPLIT_SIZES)[:-1]]
    (w_qa, w_ka, w_va, w_ga, w_qb, w_kb, w_vb, w_gb,
     w_iq, w_ik, w_iw, w_za, w_zb) = jnp.split(w_in[0], offsets, axis=1)
    pad = jnp.zeros((d, LANES - IDX_DIM - IDX_HEADS), w_in.dtype)
    w1 = jnp.concatenate([w_qa, w_ka, w_va, w_qb, w_iq, w_kb, w_vb, w_ik, w_iw, pad], axis=1).astype(BF16)
    w2 = jnp.concatenate([w_ga, w_gb, w_za, w_zb], axis=1).astype(BF16)

    qa, ka, va, qb, iq, kb, vb, ik, iw = _proj_call(x2, norm_gain, w1, _rope_tables(seq), batch, seq)

    to3 = lambda a: a.reshape(batch, seq, a.shape[-1])
    ya = _attn_a_call(qa, to3(ka), to3(va), _rel_bias_tiles(rel_bias[0]))
    yb = _attn_b_call(qb, iq, to3(iw), to3(kb), to3(vb), to3(ik), batch, seq)

    out = _out_call(x2, norm_gain, w2, b_merge[0], ya.reshape(m, A_WIDTH), yb.reshape(m, B_WIDTH),
                    w_branch_a[0].astype(BF16), w_branch_b[0].astype(BF16), w_out[0].astype(BF16),
                    final_norm_gain.reshape(1, d))
    return out.reshape(batch, seq, d)
```

```python
import jax
import jax.numpy as jnp
import numpy as np
from jax import lax
from jax.experimental import pallas as pl
from jax.experimental.pallas import tpu as pltpu

D_MODEL = 1024
CHUNK = 64
N_LEFT_CHUNKS = 8
A_HEADS = 8
A_HEAD_DIM = 64
A_WIDTH = A_HEADS * A_HEAD_DIM
REL_CLIP = 256
B_HEADS = 8
B_HEAD_DIM = 64
B_WIDTH = B_HEADS * B_HEAD_DIM
IDX_HEADS = 8
IDX_DIM = 32
TOPK_MAX = 256
ROPE_THETA = 500000.0
ROT_DIM_B = B_HEAD_DIM // 4
ROT_DIM_IDX = IDX_DIM // 4
EPS = 1e-6
NEG = -1e30

SPLIT_SIZES = (
    A_WIDTH, A_WIDTH, A_WIDTH, A_WIDTH,
    B_WIDTH, B_HEAD_DIM, B_HEAD_DIM, B_WIDTH,
    IDX_HEADS * IDX_DIM, IDX_DIM, IDX_HEADS,
    D_MODEL, D_MODEL,
)

LANES = 128
SUBLANES = 8
QB = 128
KT = 2 * QB
KEY_BITS = 32
assert KT == KEY_BITS * SUBLANES
A_BAND = QB + N_LEFT_CHUNKS * CHUNK
A_VARIANTS = N_LEFT_CHUNKS * CHUNK // QB + 1
PROJ_TM = 1024
OUT_TM = 512
VMEM_LIMIT = 48 * 1024 * 1024

BF16 = jnp.bfloat16
F32 = jnp.float32
NT_DIMS = (((1,), (1,)), ((), ()))


def _rms(x, g):
    ms = jnp.mean(x * x, axis=-1, keepdims=True)
    return (x * lax.rsqrt(ms + EPS)) * g


def _lane_iota(shape):
    return lax.broadcasted_iota(jnp.int32, shape, len(shape) - 1)


def _rope(xg, c, s, half):
    lane = _lane_iota(xg.shape)
    first = (lane % (2 * half)) < half
    partner = jnp.where(first, pltpu.roll(xg, LANES - half, 1), pltpu.roll(xg, half, 1))
    return xg * c + partner * s


def _proj_kernel(x_ref, g_ref, w_ref, tab_ref,
                 qa_ref, ka_ref, va_ref, qb_ref, iq_ref, kb_ref, vb_ref, ik_ref, iw_ref):
    xn = _rms(x_ref[...], g_ref[...])
    p = jnp.dot(xn.astype(BF16), w_ref[...], preferred_element_type=F32)
    tm = p.shape[0]
    nblk = tm // QB
    lane = _lane_iota((tm, LANES))

    a_scale = A_HEAD_DIM ** -0.5
    b_scale = B_HEAD_DIM ** -0.5
    for g in range(A_WIDTH // LANES):
        xg = p[:, g * LANES:(g + 1) * LANES] * a_scale
        for r in range(LANES // A_HEAD_DIM):
            own = (lane // A_HEAD_DIM) == r
            piece = jnp.where(own, xg, 0.0).astype(BF16)
            for j in range(nblk):
                qa_ref[j, g * (LANES // A_HEAD_DIM) + r] = piece[j * QB:(j + 1) * QB]
    ka_ref[...] = p[:, A_WIDTH:2 * A_WIDTH].astype(BF16)
    va_ref[...] = p[:, 2 * A_WIDTH:3 * A_WIDTH].astype(BF16)

    cq = tab_ref[:, 0:LANES]
    sq = tab_ref[:, LANES:2 * LANES]
    ci = tab_ref[:, 2 * LANES:3 * LANES]
    si = tab_ref[:, 3 * LANES:4 * LANES]

    base = 3 * A_WIDTH
    for g in range(B_WIDTH // LANES):
        xg = _rope(p[:, base + g * LANES: base + (g + 1) * LANES], cq, sq, ROT_DIM_B // 2) * b_scale
        per_group = LANES // B_HEAD_DIM
        for r in range(per_group):
            h = g * per_group + r
            piece = xg if r == 0 else pltpu.roll(xg, LANES - r * B_HEAD_DIM, 1)
            piece = jnp.where(lane < B_HEAD_DIM, piece, 0.0).astype(BF16)
            for j in range(nblk):
                qb_ref[j, h] = piece[j * QB:(j + 1) * QB]

    base = 3 * A_WIDTH + B_WIDTH
    for g in range(IDX_HEADS * IDX_DIM // LANES):
        xg = _rope(p[:, base + g * LANES: base + (g + 1) * LANES], ci, si, ROT_DIM_IDX // 2)
        per_group = LANES // IDX_DIM
        for r in range(per_group):
            h = g * per_group + r
            piece = xg if r == 0 else pltpu.roll(xg, LANES - r * IDX_DIM, 1)
            piece = jnp.where(lane < IDX_DIM, piece, 0.0).astype(BF16)
            for j in range(nblk):
                iq_ref[j, h] = piece[j * QB:(j + 1) * QB]

    base = 3 * A_WIDTH + B_WIDTH + IDX_HEADS * IDX_DIM
    is_k = lane < B_HEAD_DIM
    kv = _rope(p[:, base:base + LANES], jnp.where(is_k, cq, 1.0), jnp.where(is_k, sq, 0.0),
               ROT_DIM_B // 2)
    kb_ref[...] = jnp.where(is_k, kv, 0.0).astype(BF16)
    vb_ref[...] = jnp.where(is_k, pltpu.roll(kv, LANES - B_HEAD_DIM, 1), 1.0).astype(BF16)

    base = base + LANES
    is_ik = lane < IDX_DIM
    kw = _rope(p[:, base:base + LANES], jnp.where(is_ik, ci, 1.0), jnp.where(is_ik, si, 0.0),
               ROT_DIM_IDX // 2)
    ik_ref[...] = jnp.where(is_ik, kw, 0.0).astype(BF16)
    iw_scale = IDX_HEADS ** -0.5 * IDX_DIM ** -0.5
    iw_ref[...] = pltpu.roll(kw, LANES - IDX_DIM, 1) * iw_scale


def _proj_call(x2, gain, w1, tab, batch, seq):
    m = x2.shape[0]
    tm = PROJ_TM
    n_s = seq // tm
    nblk = tm // QB
    row = lambda s, b: (b * n_s + s, 0)
    blk4 = lambda s, b: (b * n_s + s, 0, 0, 0)
    wide = lambda width, dtype: jax.ShapeDtypeStruct((m, width), dtype)
    stacked = jax.ShapeDtypeStruct((m // QB, B_HEADS, QB, LANES), BF16)
    return pl.pallas_call(
        _proj_kernel,
        grid=(n_s, batch),
        in_specs=[
            pl.BlockSpec((tm, D_MODEL), row),
            pl.BlockSpec((1, D_MODEL), lambda s, b: (0, 0)),
            pl.BlockSpec(w1.shape, lambda s, b: (0, 0)),
            pl.BlockSpec((tm, 4 * LANES), lambda s, b: (s, 0)),
        ],
        out_specs=[
            pl.BlockSpec((nblk, A_HEADS, QB, LANES), blk4),
            pl.BlockSpec((tm, A_WIDTH), row),
            pl.BlockSpec((tm, A_WIDTH), row),
            pl.BlockSpec((nblk, B_HEADS, QB, LANES), blk4),
            pl.BlockSpec((nblk, IDX_HEADS, QB, LANES), blk4),
            pl.BlockSpec((tm, LANES), row),
            pl.BlockSpec((tm, LANES), row),
            pl.BlockSpec((tm, LANES), row),
            pl.BlockSpec((tm, LANES), row),
        ],
        out_shape=[
            stacked, wide(A_WIDTH, BF16), wide(A_WIDTH, BF16),
            stacked, stacked,
            wide(LANES, BF16), wide(LANES, BF16), wide(LANES, BF16), wide(LANES, F32),
        ],
        compiler_params=pltpu.CompilerParams(
            dimension_semantics=("arbitrary", "arbitrary"), vmem_limit_bytes=VMEM_LIMIT),
        name="proj",
    )(x2, gain, w1, tab)


def _attn_a_kernel(q_ref, k_ref, v_ref, bias_ref, o_ref, s_ref, p_ref, rinv_ref):
    qi = pl.program_id(1)
    start = pl.multiple_of(jnp.maximum(qi * QB - N_LEFT_CHUNKS * CHUNK, 0), QB)
    lane = _lane_iota((QB, LANES))
    per_group = LANES // A_HEAD_DIM
    group_cols = lambda h: slice((h // per_group) * LANES, (h // per_group + 1) * LANES)

    for h in range(A_HEADS):
        kg = k_ref[0, pl.ds(start, A_BAND), group_cols(h)]
        s_ref[h] = lax.dot_general(q_ref[0, h], kg, NT_DIMS, preferred_element_type=F32) + bias_ref[0, h]
    for h in range(A_HEADS):
        s = s_ref[h]
        e = jnp.exp(s - jnp.max(s, axis=-1, keepdims=True))
        p_ref[h] = e.astype(BF16)
        rinv_ref[h] = jnp.broadcast_to(1.0 / jnp.sum(e, axis=-1, keepdims=True), (QB, LANES))
    for g in range(A_WIDTH // LANES):
        out = None
        for r in range(per_group):
            h = g * per_group + r
            vg = v_ref[0, pl.ds(start, A_BAND), group_cols(h)]
            o = jnp.dot(p_ref[h], vg, preferred_element_type=F32) * rinv_ref[h]
            out = o if out is None else jnp.where((lane // A_HEAD_DIM) == r, o, out)
        o_ref[0, :, g * LANES:(g + 1) * LANES] = out


def _attn_a_call(qa, ka, va, bias):
    batch, seq, _ = ka.shape
    n_q = seq // QB
    return pl.pallas_call(
        _attn_a_kernel,
        grid=(batch, n_q),
        in_specs=[
            pl.BlockSpec((1, A_HEADS, QB, LANES), lambda b, i: (b * n_q + i, 0, 0, 0)),
            pl.BlockSpec((1, seq, A_WIDTH), lambda b, i: (b, 0, 0)),
            pl.BlockSpec((1, seq, A_WIDTH), lambda b, i: (b, 0, 0)),
            pl.BlockSpec((1, A_HEADS, QB, A_BAND),
                         lambda b, i: (jnp.minimum(i, A_VARIANTS - 1), 0, 0, 0)),
        ],
        out_specs=pl.BlockSpec((1, QB, A_WIDTH), lambda b, i: (b, i, 0)),
        out_shape=jax.ShapeDtypeStruct((batch, seq, A_WIDTH), F32),
        scratch_shapes=[
            pltpu.VMEM((A_HEADS, QB, A_BAND), F32),
            pltpu.VMEM((A_HEADS, QB, A_BAND), BF16),
            pltpu.VMEM((A_HEADS, QB, LANES), F32),
        ],
        compiler_params=pltpu.CompilerParams(
            dimension_semantics=("arbitrary", "arbitrary"), vmem_limit_bytes=VMEM_LIMIT),
        name="attn_a",
    )(qa, ka, va, bias)


def _rel_bias_tiles(rel_bias):
    pad = N_LEFT_CHUNKS * CHUNK
    width = pad + A_BAND
    length = width + QB
    n_edge = length - 1 - (2 * REL_CLIP + 1)
    assert n_edge % 2 == 0
    rb = rel_bias.astype(F32)
    h = rb.shape[0]
    f = jnp.concatenate([jnp.broadcast_to(rb[:, -1:], (h, n_edge // 2)), rb[:, ::-1],
                         jnp.broadcast_to(rb[:, :1], (h, n_edge // 2 + 1))], axis=1)
    f = jnp.roll(f, -(QB - 1), axis=1)
    toep = jnp.tile(f, (1, QB))[:, :QB * (length - 1)].reshape(h, QB, length - 1)[:, :, :width]

    cpb = QB // CHUNK
    r = np.arange(QB)
    c = np.arange(A_BAND)
    tiles = []
    for v in range(A_VARIANTS):
        c0 = v * cpb
        band_start = max(c0 - N_LEFT_CHUNKS, 0)
        delta = (c0 + r[:, None] // CHUNK) - (band_start + c[None, :] // CHUNK)
        ok = (delta >= 0) & (delta <= N_LEFT_CHUNKS)
        col0 = pad - (c0 - band_start) * CHUNK
        tiles.append(jnp.where(ok[None], toep[:, :, col0:col0 + A_BAND], NEG))
    return jnp.stack(tiles, axis=0)


def _ordered_to_float(u):
    bits = jnp.where(u < 0, u ^ jnp.int32(-2147483648), ~u)
    return pltpu.bitcast(bits, F32)


def _for_steps(n, body):
    def main(i, carry):
        body(i * 8, 8)
        return carry
    lax.fori_loop(0, n // 8, main, 0)

    @pl.when((n & 4) != 0)
    def _quad():
        body((n // 8) * 8, 4)
    done = (n // 4) * 4

    @pl.when((n & 2) != 0)
    def _pair():
        body(done, 2)

    @pl.when((n & 1) != 0)
    def _single():
        body(n - 1, 1)


def _bit_transpose(words):
    a = list(words)
    j, mask = KEY_BITS // 2, 0x0000FFFF
    while j:
        k = 0
        while k < KEY_BITS:
            t = (a[k] ^ lax.shift_right_logical(a[k + j], jnp.int32(j))) & mask
            a[k] = a[k] ^ t
            a[k + j] = a[k + j] ^ jnp.left_shift(t, jnp.int32(j))
            k = (k + j + 1) & ~j
        j >>= 1
        mask ^= (mask << j) & 0xFFFFFFFF
    return a


def _attn_b_kernel(q_ref, iq_ref, iw_ref, kb_ref, vb_ref, ik_ref, o_ref,
                   st_ref, plane_ref, sel_ref, s_ref, mrun_ref, acc_ref):
    qi = pl.program_id(1)
    n_tiles = qi + 1
    n_steps = (n_tiles + 1) // 2
    n_max = st_ref.shape[0] // KT
    rows = B_HEADS * QB
    int_min = jnp.int32(-2 ** 31)

    @pl.when(qi == 0)
    def _init_planes():
        plane_ref[...] = jnp.zeros(plane_ref.shape, jnp.int32)

    qs = q_ref[0].reshape(rows, LANES)
    iqs = iq_ref[0].reshape(rows, LANES)
    iw_t = iw_ref[0].T

    key_iota = lax.broadcasted_iota(jnp.int32, (KT, LANES), 0)
    q_lane = _lane_iota((1, LANES))
    key_limit = qi * QB + CHUNK + jnp.where(q_lane >= CHUNK, CHUNK, 0)

    def score_steps(t0, cnt):
        for g in range(cnt):
            off = pl.multiple_of((t0 + g) * KT, KT)
            lg = lax.dot_general(ik_ref[0, pl.ds(off, KT), :], iqs, NT_DIMS,
                                 preferred_element_type=F32)
            sc = None
            for h in range(IDX_HEADS):
                term = jnp.maximum(lg[:, h * QB:(h + 1) * QB], 0.0) * iw_t[h:h + 1, :]
                sc = term if sc is None else sc + term
            sc = jnp.where(key_iota + off < key_limit, sc + 0.0, -jnp.inf)
            st_ref[pl.ds(off, KT), :] = sc
            bits = pltpu.bitcast(sc, jnp.int32)
            key = bits ^ (jnp.right_shift(bits, 31) & 0x7FFFFFFF) ^ int_min
            planes = _bit_transpose([key[j * SUBLANES:(j + 1) * SUBLANES] for j in range(KEY_BITS)])
            row0 = pl.multiple_of((t0 + g) * SUBLANES, SUBLANES)
            for i in range(KEY_BITS):
                plane_ref[i, pl.ds(row0, SUBLANES), :] = planes[i]
    _for_steps(n_steps, score_steps)

    sel_ref[0:1, :] = jnp.full((1, LANES), -jnp.inf, F32)
    sel_ref[1:2, :] = jnp.full((1, LANES), -1.0, F32)

    def count(pred):
        def body(t, acc):
            off = pl.multiple_of(t * KT, KT)
            hit = jnp.where(pred(st_ref[pl.ds(off, KT), :], key_iota + off), 1.0, 0.0)
            return acc + hit[:QB] + hit[QB:]

        def body4(i, acc):
            for g in range(4):
                acc = body(i * 4 + g, acc)
            return acc
        acc = lax.fori_loop(0, n_steps // 4, body4, jnp.zeros((QB, LANES), F32))
        acc = lax.fori_loop((n_steps // 4) * 4, n_steps, body, acc)
        return jnp.sum(acc, axis=0, keepdims=True)

    k_sel = float(TOPK_MAX)

    @pl.when(n_tiles * QB - CHUNK > TOPK_MAX)
    def _search():
        step_rows = lax.broadcasted_iota(jnp.int32, (n_max * SUBLANES, LANES), 0)
        alive0 = jnp.where(step_rows < n_steps * SUBLANES, -1, 0)

        def bit_body(i, carry):
            alive, rem, key = carry
            ones = alive & plane_ref[i]
            n_ones = jnp.sum(lax.population_count(ones), axis=0, keepdims=True)
            take = n_ones >= rem
            alive = jnp.where(take, ones, alive ^ ones)
            rem = jnp.where(take, rem, rem - n_ones)
            key = key | jnp.where(take, jnp.left_shift(jnp.int32(1), KEY_BITS - 1 - i), 0)
            return alive, rem, key
        alive, rem, key = lax.fori_loop(0, KEY_BITS, bit_body,
                                  (alive0, jnp.full((1, LANES), TOPK_MAX, jnp.int32),
                                   jnp.zeros((1, LANES), jnp.int32)))
        key = key ^ int_min

        def publish(thr):
            n_ge = count(lambda s, k: s >= thr)
            sel_ref[0:1, :] = thr
            sel_ref[3:4, :] = n_ge
            sel_ref[2:3, :] = jnp.zeros((1, LANES), F32)

            @pl.when(jnp.max(jnp.abs(n_ge - k_sel)) > 0.0)
            def _count_above():
                sel_ref[2:3, :] = count(lambda s, k: s > thr)
        publish(pltpu.bitcast(key ^ (jnp.right_shift(key, 31) & 0x7FFFFFFF), F32))

        holds = (sel_ref[2:3, :] < k_sel) & (sel_ref[3:4, :] >= k_sel)

        @pl.when(jnp.min(jnp.where(holds, 1.0, 0.0)) < 1.0)
        def _float_search():
            def bit_body(i, u):
                trial = u | jnp.left_shift(jnp.int32(1), 31 - i)
                cand = _ordered_to_float(trial)
                return jnp.where(count(lambda s, k: s >= cand) >= k_sel, trial, u)
            publish(_ordered_to_float(lax.fori_loop(0, 32, bit_body, jnp.zeros((1, LANES), jnp.int32))))

        thr = sel_ref[0:1, :]
        n_ge = sel_ref[3:4, :]
        need = k_sel - sel_ref[2:3, :]
        sel_ref[1:2, :] = jnp.full((1, LANES), 2.0 ** 30, F32)

        @pl.when(jnp.max(n_ge) > k_sel)
        def _ties():
            word_key = (step_rows // SUBLANES) * KT + step_rows % SUBLANES
            ones = jnp.full(step_rows.shape, -1, jnp.int32)

            def plane_body(i, x):
                trial = x | jnp.left_shift(jnp.int32(1), 11 - i)
                n_low = jnp.clip(jnp.right_shift(trial - word_key + (SUBLANES - 1), 3), 0, KEY_BITS)
                rest = lax.shift_right_logical(ones, jnp.minimum(n_low, KEY_BITS - 1))
                below = jnp.where(n_low >= KEY_BITS, ones, ~rest)
                cnt = jnp.sum(lax.population_count(alive & below), axis=0, keepdims=True)
                return jnp.where(cnt < rem, trial, x)
            x = lax.fori_loop(0, 12, plane_body, jnp.zeros((1, LANES), jnp.int32))
            sel_ref[1:2, :] = x.astype(F32)

            n_cut = count(lambda s, k: (s == thr) & (k <= x))

            @pl.when(jnp.min(jnp.where(n_cut == need, 1.0, 0.0)) < 1.0)
            def _float_ties():
                def idx_body(i, x):
                    trial = x | jnp.left_shift(jnp.int32(1), 11 - i)
                    cnt = count(lambda s, k: (s == thr) & (k < trial))
                    return jnp.where(cnt < need, trial, x)
                x = lax.fori_loop(0, 12, idx_body, jnp.zeros((1, LANES), jnp.int32))
                sel_ref[1:2, :] = x.astype(F32)

    thr = sel_ref[0:1, :]
    cut = sel_ref[1:2, :].astype(jnp.int32)

    half_iota = key_iota[:QB]
    mrun_ref[...] = jnp.full((rows, LANES), NEG, F32)

    def pass1_steps(t0, cnt):
        best = None
        for g in range(cnt):
            off = pl.multiple_of((t0 + g) * KT, KT)
            s = lax.dot_general(qs, kb_ref[0, pl.ds(off, KT), :], NT_DIMS,
                                preferred_element_type=F32)
            halves = []
            for half in range(KT // QB):
                sc = st_ref[pl.ds(off + half * QB, QB), :]
                tie = jnp.where(half_iota + (off + half * QB) <= cut, 0.0, NEG)
                halves.append(jnp.where(sc > thr, 0.0, jnp.where(sc == thr, tie, NEG)).T)
            mask = jnp.concatenate(halves, axis=1)
            s = (s.reshape(B_HEADS, QB, KT) + mask[None]).reshape(rows, KT)
            s_ref[t0 + g] = s
            top = jnp.maximum(s[:, :LANES], s[:, LANES:])
            best = top if best is None else jnp.maximum(best, top)
        mrun_ref[...] = jnp.maximum(mrun_ref[...], best)
    _for_steps(n_steps, pass1_steps)

    m_row = jnp.max(mrun_ref[...], axis=1, keepdims=True)
    mrun_ref[...] = jnp.broadcast_to(m_row, (rows, LANES))
    acc_ref[...] = jnp.zeros((rows, LANES), F32)

    def pass2_steps(t0, cnt):
        m_b = mrun_ref[...]
        total = None
        for g in range(cnt):
            off = pl.multiple_of((t0 + g) * KT, KT)
            s = s_ref[t0 + g]
            p = jnp.concatenate([jnp.exp(s[:, :LANES] - m_b), jnp.exp(s[:, LANES:] - m_b)], axis=1)
            pv = jnp.dot(p.astype(BF16), vb_ref[0, pl.ds(off, KT), :], preferred_element_type=F32)
            total = pv if total is None else total + pv
        acc_ref[...] += total
    _for_steps(n_steps, pass2_steps)

    acc = acc_ref[...]
    out = acc / pltpu.roll(acc, B_HEAD_DIM, 1)
    lane = _lane_iota((QB, LANES))
    for g in range(B_WIDTH // LANES):
        lo = out[(2 * g) * QB:(2 * g + 1) * QB]
        hi = pltpu.roll(out[(2 * g + 1) * QB:(2 * g + 2) * QB], B_HEAD_DIM, 1)
        o_ref[0, :, g * LANES:(g + 1) * LANES] = jnp.where(lane < B_HEAD_DIM, lo, hi)


def _attn_b_call(qb, iq, iw, kb, vb, ik, batch, seq):
    n_q = seq // QB
    rows = B_HEADS * QB
    blk = lambda b, i: (b * n_q + i, 0, 0, 0)
    res = lambda b, i: (b, 0, 0)
    return pl.pallas_call(
        _attn_b_kernel,
        grid=(batch, n_q),
        in_specs=[
            pl.BlockSpec((1, B_HEADS, QB, LANES), blk),
            pl.BlockSpec((1, IDX_HEADS, QB, LANES), blk),
            pl.BlockSpec((1, QB, LANES), lambda b, i: (b, i, 0)),
            pl.BlockSpec((1, seq, LANES), res),
            pl.BlockSpec((1, seq, LANES), res),
            pl.BlockSpec((1, seq, LANES), res),
        ],
        out_specs=pl.BlockSpec((1, QB, B_WIDTH), lambda b, i: (b, i, 0)),
        out_shape=jax.ShapeDtypeStruct((batch, seq, B_WIDTH), F32),
        scratch_shapes=[
            pltpu.VMEM((seq, LANES), F32),
            pltpu.VMEM((KEY_BITS, seq // KT * SUBLANES, LANES), jnp.int32),
            pltpu.VMEM((SUBLANES, LANES), F32),
            pltpu.VMEM((seq // KT, rows, KT), F32),
            pltpu.VMEM((rows, LANES), F32),
            pltpu.VMEM((rows, LANES), F32),
        ],
        compiler_params=pltpu.CompilerParams(
            dimension_semantics=("arbitrary", "arbitrary"), vmem_limit_bytes=VMEM_LIMIT),
        name="attn_b",
    )(qb, iq, iw, kb, vb, ik)


def _sigmoid(z):
    return 1.0 / (1.0 + jnp.exp(-z))


def _out_kernel(x_ref, g_ref, wg_ref, bm_ref, ya_ref, yb_ref, wa_ref, wb_ref, wo_ref, fg_ref, o_ref):
    x = x_ref[...]
    xn = _rms(x, g_ref[...]).astype(BF16)

    def branch(y_ref, col, w_ref, bias_row):
        gate = jnp.dot(xn, wg_ref[:, col:col + A_WIDTH], preferred_element_type=F32)
        y = (y_ref[...] * (gate * _sigmoid(gate))).astype(BF16)
        pr = jnp.dot(y, w_ref[...], preferred_element_type=F32)
        zcol = 2 * A_WIDTH + bias_row * D_MODEL
        z = jnp.dot(xn, wg_ref[:, zcol:zcol + D_MODEL], preferred_element_type=F32)
        return _sigmoid(z + bm_ref[bias_row:bias_row + 1, :]) * pr

    merged = branch(ya_ref, 0, wa_ref, 0) + branch(yb_ref, A_WIDTH, wb_ref, 1)
    h = x + jnp.dot(merged.astype(BF16), wo_ref[...], preferred_element_type=F32)
    o_ref[...] = _rms(h, fg_ref[...])


def _out_call(x2, gain, w2, bm, ya, yb, wa, wb, wo, fgain):
    m = x2.shape[0]
    tm = OUT_TM
    row = lambda i: (i, 0)
    full = lambda a: pl.BlockSpec(a.shape, lambda i: (0, 0))
    return pl.pallas_call(
        _out_kernel,
        grid=(m // tm,),
        in_specs=[
            pl.BlockSpec((tm, D_MODEL), row), full(gain), full(w2), full(bm),
            pl.BlockSpec((tm, A_WIDTH), row), pl.BlockSpec((tm, B_WIDTH), row),
            full(wa), full(wb), full(wo), full(fgain),
        ],
        out_specs=pl.BlockSpec((tm, D_MODEL), row),
        out_shape=jax.ShapeDtypeStruct((m, D_MODEL), F32),
        compiler_params=pltpu.CompilerParams(
            dimension_semantics=("arbitrary",), vmem_limit_bytes=VMEM_LIMIT),
        name="out",
    )(x2, gain, w2, bm, ya, yb, wa, wb, wo, fgain)


def _rope_tables(seq):
    pos = jnp.arange(seq, dtype=F32)[:, None]

    def pattern(head_dim, rot_dim):
        half = rot_dim // 2
        inv = ROPE_THETA ** (-jnp.arange(half, dtype=F32) / half)
        ang = pos * inv[None, :]
        cos, sin = jnp.cos(ang), jnp.sin(ang)
        ones = jnp.ones((seq, head_dim - rot_dim), F32)
        c = jnp.concatenate([cos, cos, ones], axis=1)
        s = jnp.concatenate([-sin, sin, 0.0 * ones], axis=1)
        reps = LANES // head_dim
        return jnp.tile(c, (1, reps)), jnp.tile(s, (1, reps))

    cq, sq = pattern(B_HEAD_DIM, ROT_DIM_B)
    ci, si = pattern(IDX_DIM, ROT_DIM_IDX)
    return jnp.concatenate([cq, sq, ci, si], axis=1)


def kernel(x, norm_gain, w_in, b_merge, rel_bias, w_branch_a, w_branch_b, w_out, final_norm_gain):
    batch, seq, d = x.shape
    assert d == D_MODEL and seq % PROJ_TM == 0 and norm_gain.shape[0] == 1
    m = batch * seq
    x2 = x.reshape(m, d)
    offsets = [int(o) for o in np.cumsum(SPLIT_SIZES)[:-1]]
    (w_qa, w_ka, w_va, w_ga, w_qb, w_kb, w_vb, w_gb,
     w_iq, w_ik, w_iw, w_za, w_zb) = jnp.split(w_in[0], offsets, axis=1)
    pad = jnp.zeros((d, LANES - IDX_DIM - IDX_HEADS), w_in.dtype)
    w1 = jnp.concatenate([w_qa, w_ka, w_va, w_qb, w_iq, w_kb, w_vb, w_ik, w_iw, pad], axis=1).astype(BF16)
    w2 = jnp.concatenate([w_ga, w_gb, w_za, w_zb], axis=1).astype(BF16)

    qa, ka, va, qb, iq, kb, vb, ik, iw = _proj_call(x2, norm_gain, w1, _rope_tables(seq), batch, seq)

    to3 = lambda a: a.reshape(batch, seq, a.shape[-1])
    ya = _attn_a_call(qa, to3(ka), to3(va), _rel_bias_tiles(rel_bias[0]))
    yb = _attn_b_call(qb, iq, to3(iw), to3(kb), to3(vb), to3(ik), batch, seq)

    out = _out_call(x2, norm_gain, w2, b_merge[0], ya.reshape(m, A_WIDTH), yb.reshape(m, B_WIDTH),
                    w_branch_a[0].astype(BF16), w_branch_b[0].astype(BF16), w_out[0].astype(BF16),
                    final_norm_gain.reshape(1, d))
    return out.reshape(batch, seq, d)
```

```python
import jax
import jax.numpy as jnp
import numpy as np
from jax import lax
from jax.experimental import pallas as pl
from jax.experimental.pallas import tpu as pltpu

D_MODEL = 1024
CHUNK = 64
N_LEFT_CHUNKS = 8
A_HEADS = 8
A_HEAD_DIM = 64
A_WIDTH = A_HEADS * A_HEAD_DIM
REL_CLIP = 256
B_HEADS = 8
B_HEAD_DIM = 64
B_WIDTH = B_HEADS * B_HEAD_DIM
IDX_HEADS = 8
IDX_DIM = 32
TOPK_MAX = 256
ROPE_THETA = 500000.0
ROT_DIM_B = B_HEAD_DIM // 4
ROT_DIM_IDX = IDX_DIM // 4
EPS = 1e-6
NEG = -1e30

SPLIT_SIZES = (
    A_WIDTH, A_WIDTH, A_WIDTH, A_WIDTH,
    B_WIDTH, B_HEAD_DIM, B_HEAD_DIM, B_WIDTH,
    IDX_HEADS * IDX_DIM, IDX_DIM, IDX_HEADS,
    D_MODEL, D_MODEL,
)

LANES = 128
SUBLANES = 8
QB = 128
KT = 2 * QB
KEY_BITS = 32
assert KT == KEY_BITS * SUBLANES
A_BAND = QB + N_LEFT_CHUNKS * CHUNK
A_VARIANTS = N_LEFT_CHUNKS * CHUNK // QB + 1
PROJ_TM = 1024
OUT_TM = 1024
VMEM_LIMIT = 48 * 1024 * 1024
OUT_VMEM_LIMIT = 56 * 1024 * 1024

BF16 = jnp.bfloat16
F32 = jnp.float32
NT_DIMS = (((1,), (1,)), ((), ()))


def _rms(x, g):
    ms = jnp.mean(x * x, axis=-1, keepdims=True)
    return (x * lax.rsqrt(ms + EPS)) * g


def _lane_iota(shape):
    return lax.broadcasted_iota(jnp.int32, shape, len(shape) - 1)


def _rope(xg, c, s, half):
    lane = _lane_iota(xg.shape)
    first = (lane % (2 * half)) < half
    partner = jnp.where(first, pltpu.roll(xg, LANES - half, 1), pltpu.roll(xg, half, 1))
    return xg * c + partner * s


def _proj_kernel(x_ref, g_ref, w_ref, tab_ref,
                 qa_ref, ka_ref, va_ref, qb_ref, iq_ref, kb_ref, vb_ref, ik_ref, iw_ref):
    xn = _rms(x_ref[...], g_ref[...])
    p = jnp.dot(xn.astype(BF16), w_ref[...], preferred_element_type=F32)
    tm = p.shape[0]
    nblk = tm // QB
    lane = _lane_iota((tm, LANES))

    a_scale = A_HEAD_DIM ** -0.5
    b_scale = B_HEAD_DIM ** -0.5
    for g in range(A_WIDTH // LANES):
        xg = p[:, g * LANES:(g + 1) * LANES] * a_scale
        for r in range(LANES // A_HEAD_DIM):
            own = (lane // A_HEAD_DIM) == r
            piece = jnp.where(own, xg, 0.0).astype(BF16)
            for j in range(nblk):
                qa_ref[j, g * (LANES // A_HEAD_DIM) + r] = piece[j * QB:(j + 1) * QB]
    ka_ref[...] = p[:, A_WIDTH:2 * A_WIDTH].astype(BF16)
    va_ref[...] = p[:, 2 * A_WIDTH:3 * A_WIDTH].astype(BF16)

    cq = tab_ref[:, 0:LANES]
    sq = tab_ref[:, LANES:2 * LANES]
    ci = tab_ref[:, 2 * LANES:3 * LANES]
    si = tab_ref[:, 3 * LANES:4 * LANES]

    base = 3 * A_WIDTH
    for g in range(B_WIDTH // LANES):
        xg = _rope(p[:, base + g * LANES: base + (g + 1) * LANES], cq, sq, ROT_DIM_B // 2) * b_scale
        per_group = LANES // B_HEAD_DIM
        for r in range(per_group):
            h = g * per_group + r
            piece = xg if r == 0 else pltpu.roll(xg, LANES - r * B_HEAD_DIM, 1)
            piece = jnp.where(lane < B_HEAD_DIM, piece, 0.0).astype(BF16)
            for j in range(nblk):
                qb_ref[j, h] = piece[j * QB:(j + 1) * QB]

    base = 3 * A_WIDTH + B_WIDTH
    for g in range(IDX_HEADS * IDX_DIM // LANES):
        xg = _rope(p[:, base + g * LANES: base + (g + 1) * LANES], ci, si, ROT_DIM_IDX // 2)
        per_group = LANES // IDX_DIM
        for r in range(per_group):
            h = g * per_group + r
            piece = xg if r == 0 else pltpu.roll(xg, LANES - r * IDX_DIM, 1)
            piece = jnp.where(lane < IDX_DIM, piece, 0.0).astype(BF16)
            for j in range(nblk):
                iq_ref[j, h] = piece[j * QB:(j + 1) * QB]

    base = 3 * A_WIDTH + B_WIDTH + IDX_HEADS * IDX_DIM
    is_k = lane < B_HEAD_DIM
    kv = _rope(p[:, base:base + LANES], jnp.where(is_k, cq, 1.0), jnp.where(is_k, sq, 0.0),
               ROT_DIM_B // 2)
    kb_ref[...] = jnp.where(is_k, kv, 0.0).astype(BF16)
    vb_ref[...] = jnp.where(is_k, pltpu.roll(kv, LANES - B_HEAD_DIM, 1), 1.0).astype(BF16)

    base = base + LANES
    is_ik = lane < IDX_DIM
    kw = _rope(p[:, base:base + LANES], jnp.where(is_ik, ci, 1.0), jnp.where(is_ik, si, 0.0),
               ROT_DIM_IDX // 2)
    ik_ref[...] = jnp.where(is_ik, kw, 0.0).astype(BF16)
    iw_scale = IDX_HEADS ** -0.5 * IDX_DIM ** -0.5
    iw_ref[...] = pltpu.roll(kw, LANES - IDX_DIM, 1) * iw_scale


def _proj_call(x2, gain, w1, tab, batch, seq):
    m = x2.shape[0]
    tm = PROJ_TM
    n_s = seq // tm
    nblk = tm // QB
    row = lambda s, b: (b * n_s + s, 0)
    blk4 = lambda s, b: (b * n_s + s, 0, 0, 0)
    wide = lambda width, dtype: jax.ShapeDtypeStruct((m, width), dtype)
    stacked = jax.ShapeDtypeStruct((m // QB, B_HEADS, QB, LANES), BF16)
    return pl.pallas_call(
        _proj_kernel,
        grid=(n_s, batch),
        in_specs=[
            pl.BlockSpec((tm, D_MODEL), row),
            pl.BlockSpec((1, D_MODEL), lambda s, b: (0, 0)),
            pl.BlockSpec(w1.shape, lambda s, b: (0, 0)),
            pl.BlockSpec((tm, 4 * LANES), lambda s, b: (s, 0)),
        ],
        out_specs=[
            pl.BlockSpec((nblk, A_HEADS, QB, LANES), blk4),
            pl.BlockSpec((tm, A_WIDTH), row),
            pl.BlockSpec((tm, A_WIDTH), row),
            pl.BlockSpec((nblk, B_HEADS, QB, LANES), blk4),
            pl.BlockSpec((nblk, IDX_HEADS, QB, LANES), blk4),
            pl.BlockSpec((tm, LANES), row),
            pl.BlockSpec((tm, LANES), row),
            pl.BlockSpec((tm, LANES), row),
            pl.BlockSpec((tm, LANES), row),
        ],
        out_shape=[
            stacked, wide(A_WIDTH, BF16), wide(A_WIDTH, BF16),
            stacked, stacked,
            wide(LANES, BF16), wide(LANES, BF16), wide(LANES, BF16), wide(LANES, F32),
        ],
        compiler_params=pltpu.CompilerParams(
            dimension_semantics=("arbitrary", "arbitrary"), vmem_limit_bytes=VMEM_LIMIT),
        name="proj",
    )(x2, gain, w1, tab)


def _attn_a_kernel(q_ref, k_ref, v_ref, bias_ref, o_ref, s_ref, p_ref, rinv_ref):
    qi = pl.program_id(1)
    start = pl.multiple_of(jnp.maximum(qi * QB - N_LEFT_CHUNKS * CHUNK, 0), QB)
    lane = _lane_iota((QB, LANES))
    per_group = LANES // A_HEAD_DIM
    group_cols = lambda h: slice((h // per_group) * LANES, (h // per_group + 1) * LANES)

    for h in range(A_HEADS):
        kg = k_ref[0, pl.ds(start, A_BAND), group_cols(h)]
        s_ref[h] = lax.dot_general(q_ref[0, h], kg, NT_DIMS, preferred_element_type=F32) + bias_ref[0, h]
    for h in range(A_HEADS):
        s = s_ref[h]
        e = jnp.exp(s - jnp.max(s, axis=-1, keepdims=True))
        p_ref[h] = e.astype(BF16)
        rinv_ref[h] = jnp.broadcast_to(1.0 / jnp.sum(e, axis=-1, keepdims=True), (QB, LANES))
    for g in range(A_WIDTH // LANES):
        out = None
        for r in range(per_group):
            h = g * per_group + r
            vg = v_ref[0, pl.ds(start, A_BAND), group_cols(h)]
            o = jnp.dot(p_ref[h], vg, preferred_element_type=F32) * rinv_ref[h]
            out = o if out is None else jnp.where((lane // A_HEAD_DIM) == r, o, out)
        o_ref[0, :, g * LANES:(g + 1) * LANES] = out


def _attn_a_call(qa, ka, va, bias):
    batch, seq, _ = ka.shape
    n_q = seq // QB
    return pl.pallas_call(
        _attn_a_kernel,
        grid=(batch, n_q),
        in_specs=[
            pl.BlockSpec((1, A_HEADS, QB, LANES), lambda b, i: (b * n_q + i, 0, 0, 0)),
            pl.BlockSpec((1, seq, A_WIDTH), lambda b, i: (b, 0, 0)),
            pl.BlockSpec((1, seq, A_WIDTH), lambda b, i: (b, 0, 0)),
            pl.BlockSpec((1, A_HEADS, QB, A_BAND),
                         lambda b, i: (jnp.minimum(i, A_VARIANTS - 1), 0, 0, 0)),
        ],
        out_specs=pl.BlockSpec((1, QB, A_WIDTH), lambda b, i: (b, i, 0)),
        out_shape=jax.ShapeDtypeStruct((batch, seq, A_WIDTH), F32),
        scratch_shapes=[
            pltpu.VMEM((A_HEADS, QB, A_BAND), F32),
            pltpu.VMEM((A_HEADS, QB, A_BAND), BF16),
            pltpu.VMEM((A_HEADS, QB, LANES), F32),
        ],
        compiler_params=pltpu.CompilerParams(
            dimension_semantics=("arbitrary", "arbitrary"), vmem_limit_bytes=VMEM_LIMIT),
        name="attn_a",
    )(qa, ka, va, bias)


def _rel_bias_tiles(rel_bias):
    pad = N_LEFT_CHUNKS * CHUNK
    width = pad + A_BAND
    length = width + QB
    n_edge = length - 1 - (2 * REL_CLIP + 1)
    assert n_edge % 2 == 0
    rb = rel_bias.astype(F32)
    h = rb.shape[0]
    f = jnp.concatenate([jnp.broadcast_to(rb[:, -1:], (h, n_edge // 2)), rb[:, ::-1],
                         jnp.broadcast_to(rb[:, :1], (h, n_edge // 2 + 1))], axis=1)
    f = jnp.roll(f, -(QB - 1), axis=1)
    toep = jnp.tile(f, (1, QB))[:, :QB * (length - 1)].reshape(h, QB, length - 1)[:, :, :width]

    cpb = QB // CHUNK
    r = np.arange(QB)
    c = np.arange(A_BAND)
    tiles = []
    for v in range(A_VARIANTS):
        c0 = v * cpb
        band_start = max(c0 - N_LEFT_CHUNKS, 0)
        delta = (c0 + r[:, None] // CHUNK) - (band_start + c[None, :] // CHUNK)
        ok = (delta >= 0) & (delta <= N_LEFT_CHUNKS)
        col0 = pad - (c0 - band_start) * CHUNK
        tiles.append(jnp.where(ok[None], toep[:, :, col0:col0 + A_BAND], NEG))
    return jnp.stack(tiles, axis=0)


def _ordered_to_float(u):
    bits = jnp.where(u < 0, u ^ jnp.int32(-2147483648), ~u)
    return pltpu.bitcast(bits, F32)


def _for_steps(n, body):
    def main(i, carry):
        body(i * 8, 8)
        return carry
    lax.fori_loop(0, n // 8, main, 0)

    @pl.when((n & 4) != 0)
    def _quad():
        body((n // 8) * 8, 4)
    done = (n // 4) * 4

    @pl.when((n & 2) != 0)
    def _pair():
        body(done, 2)

    @pl.when((n & 1) != 0)
    def _single():
        body(n - 1, 1)


def _bit_transpose(words):
    a = list(words)
    j, mask = KEY_BITS // 2, 0x0000FFFF
    while j:
        k = 0
        while k < KEY_BITS:
            t = (a[k] ^ lax.shift_right_logical(a[k + j], jnp.int32(j))) & mask
            a[k] = a[k] ^ t
            a[k + j] = a[k + j] ^ jnp.left_shift(t, jnp.int32(j))
            k = (k + j + 1) & ~j
        j >>= 1
        mask ^= (mask << j) & 0xFFFFFFFF
    return a


def _attn_b_kernel(q_ref, iq_ref, iw_ref, kb_ref, vb_ref, ik_ref, o_ref,
                   st_ref, plane_ref, sel_ref, s_ref, mrun_ref, acc_ref):
    qi = pl.program_id(1)
    n_tiles = qi + 1
    n_steps = (n_tiles + 1) // 2
    n_max = st_ref.shape[0] // KT
    rows = B_HEADS * QB
    int_min = jnp.int32(-2 ** 31)

    @pl.when(qi == 0)
    def _init_planes():
        plane_ref[...] = jnp.zeros(plane_ref.shape, jnp.int32)

    qs = q_ref[0].reshape(rows, LANES)
    iqs = iq_ref[0].reshape(rows, LANES)
    iw_t = iw_ref[0].T

    key_iota = lax.broadcasted_iota(jnp.int32, (KT, LANES), 0)
    q_lane = _lane_iota((1, LANES))
    key_limit = qi * QB + CHUNK + jnp.where(q_lane >= CHUNK, CHUNK, 0)

    def score_steps(t0, cnt):
        for g in range(cnt):
            off = pl.multiple_of((t0 + g) * KT, KT)
            lg = lax.dot_general(ik_ref[0, pl.ds(off, KT), :], iqs, NT_DIMS,
                                 preferred_element_type=F32)
            sc = None
            for h in range(IDX_HEADS):
                term = jnp.maximum(lg[:, h * QB:(h + 1) * QB], 0.0) * iw_t[h:h + 1, :]
                sc = term if sc is None else sc + term
            sc = jnp.where(key_iota + off < key_limit, sc + 0.0, -jnp.inf)
            st_ref[pl.ds(off, KT), :] = sc
            bits = pltpu.bitcast(sc, jnp.int32)
            key = bits ^ (jnp.right_shift(bits, 31) & 0x7FFFFFFF) ^ int_min
            planes = _bit_transpose([key[j * SUBLANES:(j + 1) * SUBLANES] for j in range(KEY_BITS)])
            row0 = pl.multiple_of((t0 + g) * SUBLANES, SUBLANES)
            for i in range(KEY_BITS):
                plane_ref[i, pl.ds(row0, SUBLANES), :] = planes[i]
    _for_steps(n_steps, score_steps)

    sel_ref[0:1, :] = jnp.full((1, LANES), -jnp.inf, F32)
    sel_ref[1:2, :] = jnp.full((1, LANES), -1.0, F32)

    def count(pred):
        def body(t, acc):
            off = pl.multiple_of(t * KT, KT)
            hit = jnp.where(pred(st_ref[pl.ds(off, KT), :], key_iota + off), 1.0, 0.0)
            return acc + hit[:QB] + hit[QB:]

        def body4(i, acc):
            for g in range(4):
                acc = body(i * 4 + g, acc)
            return acc
        acc = lax.fori_loop(0, n_steps // 4, body4, jnp.zeros((QB, LANES), F32))
        acc = lax.fori_loop((n_steps // 4) * 4, n_steps, body, acc)
        return jnp.sum(acc, axis=0, keepdims=True)

    k_sel = float(TOPK_MAX)

    @pl.when(n_tiles * QB - CHUNK > TOPK_MAX)
    def _search():
        step_rows = lax.broadcasted_iota(jnp.int32, (n_max * SUBLANES, LANES), 0)
        alive0 = jnp.where(step_rows < n_steps * SUBLANES, -1, 0)

        def bit_body(i, carry):
            alive, rem, key = carry
            ones = alive & plane_ref[i]
            n_ones = jnp.sum(lax.population_count(ones), axis=0, keepdims=True)
            take = n_ones >= rem
            alive = jnp.where(take, ones, alive ^ ones)
            rem = jnp.where(take, rem, rem - n_ones)
            key = key | jnp.where(take, jnp.left_shift(jnp.int32(1), KEY_BITS - 1 - i), 0)
            return alive, rem, key
        alive, rem, key = lax.fori_loop(0, KEY_BITS, bit_body,
                                  (alive0, jnp.full((1, LANES), TOPK_MAX, jnp.int32),
                                   jnp.zeros((1, LANES), jnp.int32)))
        key = key ^ int_min

        def publish(thr):
            n_ge = count(lambda s, k: s >= thr)
            sel_ref[0:1, :] = thr
            sel_ref[3:4, :] = n_ge
            sel_ref[2:3, :] = jnp.zeros((1, LANES), F32)

            @pl.when(jnp.max(jnp.abs(n_ge - k_sel)) > 0.0)
            def _count_above():
                sel_ref[2:3, :] = count(lambda s, k: s > thr)
        publish(pltpu.bitcast(key ^ (jnp.right_shift(key, 31) & 0x7FFFFFFF), F32))

        holds = (sel_ref[2:3, :] < k_sel) & (sel_ref[3:4, :] >= k_sel)

        @pl.when(jnp.min(jnp.where(holds, 1.0, 0.0)) < 1.0)
        def _float_search():
            def bit_body(i, u):
                trial = u | jnp.left_shift(jnp.int32(1), 31 - i)
                cand = _ordered_to_float(trial)
                return jnp.where(count(lambda s, k: s >= cand) >= k_sel, trial, u)
            publish(_ordered_to_float(lax.fori_loop(0, 32, bit_body, jnp.zeros((1, LANES), jnp.int32))))

        thr = sel_ref[0:1, :]
        n_ge = sel_ref[3:4, :]
        need = k_sel - sel_ref[2:3, :]
        sel_ref[1:2, :] = jnp.full((1, LANES), 2.0 ** 30, F32)

        @pl.when(jnp.max(n_ge) > k_sel)
        def _ties():
            word_key = (step_rows // SUBLANES) * KT + step_rows % SUBLANES
            ones = jnp.full(step_rows.shape, -1, jnp.int32)

            def plane_body(i, x):
                trial = x | jnp.left_shift(jnp.int32(1), 11 - i)
                n_low = jnp.clip(jnp.right_shift(trial - word_key + (SUBLANES - 1), 3), 0, KEY_BITS)
                rest = lax.shift_right_logical(ones, jnp.minimum(n_low, KEY_BITS - 1))
                below = jnp.where(n_low >= KEY_BITS, ones, ~rest)
                cnt = jnp.sum(lax.population_count(alive & below), axis=0, keepdims=True)
                return jnp.where(cnt < rem, trial, x)
            x = lax.fori_loop(0, 12, plane_body, jnp.zeros((1, LANES), jnp.int32))
            sel_ref[1:2, :] = x.astype(F32)

            n_cut = count(lambda s, k: (s == thr) & (k <= x))

            @pl.when(jnp.min(jnp.where(n_cut == need, 1.0, 0.0)) < 1.0)
            def _float_ties():
                def idx_body(i, x):
                    trial = x | jnp.left_shift(jnp.int32(1), 11 - i)
                    cnt = count(lambda s, k: (s == thr) & (k < trial))
                    return jnp.where(cnt < need, trial, x)
                x = lax.fori_loop(0, 12, idx_body, jnp.zeros((1, LANES), jnp.int32))
                sel_ref[1:2, :] = x.astype(F32)

    thr = sel_ref[0:1, :]
    cut = sel_ref[1:2, :].astype(jnp.int32)

    half_iota = key_iota[:QB]
    mrun_ref[...] = jnp.full((rows, LANES), NEG, F32)

    def pass1_steps(t0, cnt):
        best = None
        for g in range(cnt):
            off = pl.multiple_of((t0 + g) * KT, KT)
            s = lax.dot_general(qs, kb_ref[0, pl.ds(off, KT), :], NT_DIMS,
                                preferred_element_type=F32)
            halves = []
            for half in range(KT // QB):
                sc = st_ref[pl.ds(off + half * QB, QB), :]
                tie = jnp.where(half_iota + (off + half * QB) <= cut, 0.0, NEG)
                halves.append(jnp.where(sc > thr, 0.0, jnp.where(sc == thr, tie, NEG)).T)
            mask = jnp.concatenate(halves, axis=1)
            s = (s.reshape(B_HEADS, QB, KT) + mask[None]).reshape(rows, KT)
            s_ref[t0 + g] = s
            top = jnp.maximum(s[:, :LANES], s[:, LANES:])
            best = top if best is None else jnp.maximum(best, top)
        mrun_ref[...] = jnp.maximum(mrun_ref[...], best)
    _for_steps(n_steps, pass1_steps)

    m_row = jnp.max(mrun_ref[...], axis=1, keepdims=True)
    mrun_ref[...] = jnp.broadcast_to(m_row, (rows, LANES))
    acc_ref[...] = jnp.zeros((rows, LANES), F32)

    def pass2_steps(t0, cnt):
        m_b = mrun_ref[...]
        probs = []
        for g in range(cnt):
            s = s_ref[t0 + g]
            for c in range(KT // LANES):
                probs.append(jnp.exp(s[:, c * LANES:(c + 1) * LANES] - m_b).astype(BF16))
        off = pl.multiple_of(t0 * KT, KT)
        acc_ref[...] += jnp.dot(jnp.concatenate(probs, axis=1), vb_ref[0, pl.ds(off, cnt * KT), :],
                                preferred_element_type=F32)
    _for_steps(n_steps, pass2_steps)

    lane = _lane_iota((QB, LANES))
    low = lane < B_HEAD_DIM
    for g in range(B_WIDTH // LANES):
        even = acc_ref[(2 * g) * QB:(2 * g + 1) * QB, :]
        odd = acc_ref[(2 * g + 1) * QB:(2 * g + 2) * QB, :]
        numer = jnp.where(low, even, pltpu.roll(odd, B_HEAD_DIM, 1))
        denom = jnp.where(low, pltpu.roll(even, B_HEAD_DIM, 1), odd)
        o_ref[0, :, g * LANES:(g + 1) * LANES] = numer / denom


def _attn_b_call(qb, iq, iw, kb, vb, ik, batch, seq):
    n_q = seq // QB
    rows = B_HEADS * QB
    blk = lambda b, i: (b * n_q + i, 0, 0, 0)
    res = lambda b, i: (b, 0, 0)
    return pl.pallas_call(
        _attn_b_kernel,
        grid=(batch, n_q),
        in_specs=[
            pl.BlockSpec((1, B_HEADS, QB, LANES), blk),
            pl.BlockSpec((1, IDX_HEADS, QB, LANES), blk),
            pl.BlockSpec((1, QB, LANES), lambda b, i: (b, i, 0)),
            pl.BlockSpec((1, seq, LANES), res),
            pl.BlockSpec((1, seq, LANES), res),
            pl.BlockSpec((1, seq, LANES), res),
        ],
        out_specs=pl.BlockSpec((1, QB, B_WIDTH), lambda b, i: (b, i, 0)),
        out_shape=jax.ShapeDtypeStruct((batch, seq, B_WIDTH), F32),
        scratch_shapes=[
            pltpu.VMEM((seq, LANES), F32),
            pltpu.VMEM((KEY_BITS, seq // KT * SUBLANES, LANES), jnp.int32),
            pltpu.VMEM((SUBLANES, LANES), F32),
            pltpu.VMEM((seq // KT, rows, KT), F32),
            pltpu.VMEM((rows, LANES), F32),
            pltpu.VMEM((rows, LANES), F32),
        ],
        compiler_params=pltpu.CompilerParams(
            dimension_semantics=("arbitrary", "arbitrary"), vmem_limit_bytes=VMEM_LIMIT),
        name="attn_b",
    )(qb, iq, iw, kb, vb, ik)


def _sigmoid(z):
    return 1.0 / (1.0 + jnp.exp(-z))


def _out_kernel(x_ref, g_ref, wg_ref, bm_ref, ya_ref, yb_ref, wa_ref, wb_ref, wo_ref, fg_ref, o_ref):
    x = x_ref[...]
    xn = _rms(x, g_ref[...]).astype(BF16)

    def branch(y_ref, col, w_ref, bias_row):
        gate = jnp.dot(xn, wg_ref[:, col:col + A_WIDTH], preferred_element_type=F32)
        y = (y_ref[...] * (gate * _sigmoid(gate))).astype(BF16)
        pr = jnp.dot(y, w_ref[...], preferred_element_type=F32)
        zcol = 2 * A_WIDTH + bias_row * D_MODEL
        z = jnp.dot(xn, wg_ref[:, zcol:zcol + D_MODEL], preferred_element_type=F32)
        return _sigmoid(z + bm_ref[bias_row:bias_row + 1, :]) * pr

    merged = branch(ya_ref, 0, wa_ref, 0) + branch(yb_ref, A_WIDTH, wb_ref, 1)
    h = x + jnp.dot(merged.astype(BF16), wo_ref[...], preferred_element_type=F32)
    o_ref[...] = _rms(h, fg_ref[...])


def _out_call(x2, gain, w2, bm, ya, yb, wa, wb, wo, fgain):
    m = x2.shape[0]
    tm = OUT_TM
    row = lambda i: (i, 0)
    full = lambda a: pl.BlockSpec(a.shape, lambda i: (0, 0), pipeline_mode=pl.Buffered(1))
    return pl.pallas_call(
        _out_kernel,
        grid=(m // tm,),
        in_specs=[
            pl.BlockSpec((tm, D_MODEL), row), full(gain), full(w2), full(bm),
            pl.BlockSpec((tm, A_WIDTH), row), pl.BlockSpec((tm, B_WIDTH), row),
            full(wa), full(wb), full(wo), full(fgain),
        ],
        out_specs=pl.BlockSpec((tm, D_MODEL), row),
        out_shape=jax.ShapeDtypeStruct((m, D_MODEL), F32),
        compiler_params=pltpu.CompilerParams(
            dimension_semantics=("arbitrary",), vmem_limit_bytes=OUT_VMEM_LIMIT),
        name="out",
    )(x2, gain, w2, bm, ya, yb, wa, wb, wo, fgain)


def _rope_tables(seq):
    pos = jnp.arange(seq, dtype=F32)[:, None]

    def pattern(head_dim, rot_dim):
        half = rot_dim // 2
        inv = ROPE_THETA ** (-jnp.arange(half, dtype=F32) / half)
        ang = pos * inv[None, :]
        cos, sin = jnp.cos(ang), jnp.sin(ang)
        ones = jnp.ones((seq, head_dim - rot_dim), F32)
        c = jnp.concatenate([cos, cos, ones], axis=1)
        s = jnp.concatenate([-sin, sin, 0.0 * ones], axis=1)
        reps = LANES // head_dim
        return jnp.tile(c, (1, reps)), jnp.tile(s, (1, reps))

    cq, sq = pattern(B_HEAD_DIM, ROT_DIM_B)
    ci, si = pattern(IDX_DIM, ROT_DIM_IDX)
    return jnp.concatenate([cq, sq, ci, si], axis=1)


def kernel(x, norm_gain, w_in, b_merge, rel_bias, w_branch_a, w_branch_b, w_out, final_norm_gain):
    batch, seq, d = x.shape
    assert d == D_MODEL and seq % PROJ_TM == 0 and norm_gain.shape[0] == 1
    m = batch * seq
    x2 = x.reshape(m, d)
    offsets = [int(o) for o in np.cumsum(SPLIT_SIZES)[:-1]]
    (w_qa, w_ka, w_va, w_ga, w_qb, w_kb, w_vb, w_gb,
     w_iq, w_ik, w_iw, w_za, w_zb) = jnp.split(w_in[0], offsets, axis=1)
    pad = jnp.zeros((d, LANES - IDX_DIM - IDX_HEADS), w_in.dtype)
    w1 = jnp.concatenate([w_qa, w_ka, w_va, w_qb, w_iq, w_kb, w_vb, w_ik, w_iw, pad], axis=1).astype(BF16)
    w2 = jnp.concatenate([w_ga, w_gb, w_za, w_zb], axis=1).astype(BF16)

    qa, ka, va, qb, iq, kb, vb, ik, iw = _proj_call(x2, norm_gain, w1, _rope_tables(seq), batch, seq)

    to3 = lambda a: a.reshape(batch, seq, a.shape[-1])
    ya = _attn_a_call(qa, to3(ka), to3(va), _rel_bias_tiles(rel_bias[0]))
    yb = _attn_b_call(qb, iq, to3(iw), to3(kb), to3(vb), to3(ik), batch, seq)

    out = _out_call(x2, norm_gain, w2, b_merge[0], ya.reshape(m, A_WIDTH), yb.reshape(m, B_WIDTH),
                    w_branch_a[0].astype(BF16), w_branch_b[0].astype(BF16), w_out[0].astype(BF16),
                    final_norm_gain.reshape(1, d))
    return out.reshape(batch, seq, d)
```

```python
import jax
import jax.numpy as jnp
import numpy as np
from jax import lax
from jax.experimental import pallas as pl
from jax.experimental.pallas import tpu as pltpu

D_MODEL = 1024
CHUNK = 64
N_LEFT_CHUNKS = 8
A_HEADS = 8
A_HEAD_DIM = 64
A_WIDTH = A_HEADS * A_HEAD_DIM
REL_CLIP = 256
B_HEADS = 8
B_HEAD_DIM = 64
B_WIDTH = B_HEADS * B_HEAD_DIM
IDX_HEADS = 8
IDX_DIM = 32
TOPK_MAX = 256
ROPE_THETA = 500000.0
ROT_DIM_B = B_HEAD_DIM // 4
ROT_DIM_IDX = IDX_DIM // 4
EPS = 1e-6
NEG = -1e30

SPLIT_SIZES = (
    A_WIDTH, A_WIDTH, A_WIDTH, A_WIDTH,
    B_WIDTH, B_HEAD_DIM, B_HEAD_DIM, B_WIDTH,
    IDX_HEADS * IDX_DIM, IDX_DIM, IDX_HEADS,
    D_MODEL, D_MODEL,
)

LANES = 128
SUBLANES = 8
QB = 128
KT = 2 * QB
KEY_BITS = 32
assert KT == KEY_BITS * SUBLANES
A_BAND = QB + N_LEFT_CHUNKS * CHUNK
A_VARIANTS = N_LEFT_CHUNKS * CHUNK // QB + 1
PROJ_TM = 1024
OUT_TM = 1024
VMEM_LIMIT = 48 * 1024 * 1024
OUT_VMEM_LIMIT = 56 * 1024 * 1024

BF16 = jnp.bfloat16
F32 = jnp.float32
NT_DIMS = (((1,), (1,)), ((), ()))


def _rms(x, g):
    ms = jnp.mean(x * x, axis=-1, keepdims=True)
    return (x * lax.rsqrt(ms + EPS)) * g


def _lane_iota(shape):
    return lax.broadcasted_iota(jnp.int32, shape, len(shape) - 1)


def _rope(xg, c, s, half):
    lane = _lane_iota(xg.shape)
    first = (lane % (2 * half)) < half
    partner = jnp.where(first, pltpu.roll(xg, LANES - half, 1), pltpu.roll(xg, half, 1))
    return xg * c + partner * s


def _proj_kernel(x_ref, g_ref, w_ref, tab_ref,
                 qa_ref, ka_ref, va_ref, qb_ref, iq_ref, kb_ref, vb_ref, ik_ref, iw_ref):
    xn = _rms(x_ref[...], g_ref[...])
    p = jnp.dot(xn.astype(BF16), w_ref[...], preferred_element_type=F32)
    tm = p.shape[0]
    nblk = tm // QB
    lane = _lane_iota((tm, LANES))

    a_scale = A_HEAD_DIM ** -0.5
    b_scale = B_HEAD_DIM ** -0.5
    for g in range(A_WIDTH // LANES):
        xg = p[:, g * LANES:(g + 1) * LANES] * a_scale
        for r in range(LANES // A_HEAD_DIM):
            own = (lane // A_HEAD_DIM) == r
            piece = jnp.where(own, xg, 0.0).astype(BF16)
            for j in range(nblk):
                qa_ref[j, g * (LANES // A_HEAD_DIM) + r] = piece[j * QB:(j + 1) * QB]
    ka_ref[...] = p[:, A_WIDTH:2 * A_WIDTH].astype(BF16)
    va_ref[...] = p[:, 2 * A_WIDTH:3 * A_WIDTH].astype(BF16)

    cq = tab_ref[:, 0:LANES]
    sq = tab_ref[:, LANES:2 * LANES]
    ci = tab_ref[:, 2 * LANES:3 * LANES]
    si = tab_ref[:, 3 * LANES:4 * LANES]

    base = 3 * A_WIDTH
    for g in range(B_WIDTH // LANES):
        xg = _rope(p[:, base + g * LANES: base + (g + 1) * LANES], cq, sq, ROT_DIM_B // 2) * b_scale
        per_group = LANES // B_HEAD_DIM
        for r in range(per_group):
            h = g * per_group + r
            piece = xg if r == 0 else pltpu.roll(xg, LANES - r * B_HEAD_DIM, 1)
            piece = jnp.where(lane < B_HEAD_DIM, piece, 0.0).astype(BF16)
            for j in range(nblk):
                qb_ref[j, h] = piece[j * QB:(j + 1) * QB]

    base = 3 * A_WIDTH + B_WIDTH
    for g in range(IDX_HEADS * IDX_DIM // LANES):
        xg = _rope(p[:, base + g * LANES: base + (g + 1) * LANES], ci, si, ROT_DIM_IDX // 2)
        per_group = LANES // IDX_DIM
        for r in range(per_group):
            h = g * per_group + r
            piece = xg if r == 0 else pltpu.roll(xg, LANES - r * IDX_DIM, 1)
            piece = jnp.where(lane < IDX_DIM, piece, 0.0).astype(BF16)
            for j in range(nblk):
                iq_ref[j, h] = piece[j * QB:(j + 1) * QB]

    base = 3 * A_WIDTH + B_WIDTH + IDX_HEADS * IDX_DIM
    is_k = lane < B_HEAD_DIM
    kv = _rope(p[:, base:base + LANES], jnp.where(is_k, cq, 1.0), jnp.where(is_k, sq, 0.0),
               ROT_DIM_B // 2)
    kb_ref[...] = jnp.where(is_k, kv, 0.0).astype(BF16)
    vb_ref[...] = jnp.where(is_k, pltpu.roll(kv, LANES - B_HEAD_DIM, 1), 1.0).astype(BF16)

    base = base + LANES
    is_ik = lane < IDX_DIM
    kw = _rope(p[:, base:base + LANES], jnp.where(is_ik, ci, 1.0), jnp.where(is_ik, si, 0.0),
               ROT_DIM_IDX // 2)
    ik_ref[...] = jnp.where(is_ik, kw, 0.0).astype(BF16)
    iw_scale = IDX_HEADS ** -0.5 * IDX_DIM ** -0.5
    iw_ref[...] = pltpu.roll(kw, LANES - IDX_DIM, 1) * iw_scale


def _proj_call(x2, gain, w1, tab, batch, seq):
    m = x2.shape[0]
    tm = PROJ_TM
    n_s = seq // tm
    nblk = tm // QB
    row = lambda s, b: (b * n_s + s, 0)
    blk4 = lambda s, b: (b * n_s + s, 0, 0, 0)
    wide = lambda width, dtype: jax.ShapeDtypeStruct((m, width), dtype)
    stacked = jax.ShapeDtypeStruct((m // QB, B_HEADS, QB, LANES), BF16)
    return pl.pallas_call(
        _proj_kernel,
        grid=(n_s, batch),
        in_specs=[
            pl.BlockSpec((tm, D_MODEL), row),
            pl.BlockSpec((1, D_MODEL), lambda s, b: (0, 0)),
            pl.BlockSpec(w1.shape, lambda s, b: (0, 0)),
            pl.BlockSpec((tm, 4 * LANES), lambda s, b: (s, 0)),
        ],
        out_specs=[
            pl.BlockSpec((nblk, A_HEADS, QB, LANES), blk4),
            pl.BlockSpec((tm, A_WIDTH), row),
            pl.BlockSpec((tm, A_WIDTH), row),
            pl.BlockSpec((nblk, B_HEADS, QB, LANES), blk4),
            pl.BlockSpec((nblk, IDX_HEADS, QB, LANES), blk4),
            pl.BlockSpec((tm, LANES), row),
            pl.BlockSpec((tm, LANES), row),
            pl.BlockSpec((tm, LANES), row),
            pl.BlockSpec((tm, LANES), row),
        ],
        out_shape=[
            stacked, wide(A_WIDTH, BF16), wide(A_WIDTH, BF16),
            stacked, stacked,
            wide(LANES, BF16), wide(LANES, BF16), wide(LANES, BF16), wide(LANES, F32),
        ],
        compiler_params=pltpu.CompilerParams(
            dimension_semantics=("arbitrary", "arbitrary"), vmem_limit_bytes=VMEM_LIMIT),
        name="proj",
    )(x2, gain, w1, tab)


A_SUB = 2


def _attn_a_kernel(q_ref, k_ref, v_ref, bias0_ref, bias1_ref, o_ref, s_ref, p_ref, rinv_ref):
    qi = pl.program_id(1)
    lane = _lane_iota((QB, LANES))
    per_group = LANES // A_HEAD_DIM
    group_cols = lambda h: slice((h // per_group) * LANES, (h // per_group + 1) * LANES)
    bias_refs = (bias0_ref, bias1_ref)
    starts = [pl.multiple_of(jnp.maximum((qi * A_SUB + u) * QB - N_LEFT_CHUNKS * CHUNK, 0), QB)
              for u in range(A_SUB)]

    for u in range(A_SUB):
        for h in range(A_HEADS):
            kg = k_ref[0, pl.ds(starts[u], A_BAND), group_cols(h)]
            s_ref[u, h] = (lax.dot_general(q_ref[u, h], kg, NT_DIMS, preferred_element_type=F32)
                           + bias_refs[u][0, h])
    for u in range(A_SUB):
        for h in range(A_HEADS):
            s = s_ref[u, h]
            e = jnp.exp(s - jnp.max(s, axis=-1, keepdims=True))
            p_ref[u, h] = e.astype(BF16)
            rinv_ref[u, h] = jnp.broadcast_to(1.0 / jnp.sum(e, axis=-1, keepdims=True), (QB, LANES))
    for u in range(A_SUB):
        for g in range(A_WIDTH // LANES):
            out = None
            for r in range(per_group):
                h = g * per_group + r
                vg = v_ref[0, pl.ds(starts[u], A_BAND), group_cols(h)]
                o = jnp.dot(p_ref[u, h], vg, preferred_element_type=F32) * rinv_ref[u, h]
                out = o if out is None else jnp.where((lane // A_HEAD_DIM) == r, o, out)
            o_ref[0, u * QB:(u + 1) * QB, g * LANES:(g + 1) * LANES] = out


def _attn_a_call(qa, ka, va, bias):
    batch, seq, _ = ka.shape
    n_q = seq // (QB * A_SUB)
    assert seq % (QB * A_SUB) == 0 and A_SUB == 2
    variant = lambda blk: jnp.minimum(blk, A_VARIANTS - 1)
    return pl.pallas_call(
        _attn_a_kernel,
        grid=(batch, n_q),
        in_specs=[
            pl.BlockSpec((A_SUB, A_HEADS, QB, LANES), lambda b, i: (b * n_q + i, 0, 0, 0)),
            pl.BlockSpec((1, seq, A_WIDTH), lambda b, i: (b, 0, 0)),
            pl.BlockSpec((1, seq, A_WIDTH), lambda b, i: (b, 0, 0)),
            pl.BlockSpec((1, A_HEADS, QB, A_BAND), lambda b, i: (variant(A_SUB * i), 0, 0, 0)),
            pl.BlockSpec((1, A_HEADS, QB, A_BAND), lambda b, i: (variant(A_SUB * i + 1), 0, 0, 0)),
        ],
        out_specs=pl.BlockSpec((1, A_SUB * QB, A_WIDTH), lambda b, i: (b, i, 0)),
        out_shape=jax.ShapeDtypeStruct((batch, seq, A_WIDTH), F32),
        scratch_shapes=[
            pltpu.VMEM((A_SUB, A_HEADS, QB, A_BAND), F32),
            pltpu.VMEM((A_SUB, A_HEADS, QB, A_BAND), BF16),
            pltpu.VMEM((A_SUB, A_HEADS, QB, LANES), F32),
        ],
        compiler_params=pltpu.CompilerParams(
            dimension_semantics=("arbitrary", "arbitrary"), vmem_limit_bytes=VMEM_LIMIT),
        name="attn_a",
    )(qa, ka, va, bias, bias)


def _rel_bias_tiles(rel_bias):
    pad = N_LEFT_CHUNKS * CHUNK
    width = pad + A_BAND
    length = width + QB
    n_edge = length - 1 - (2 * REL_CLIP + 1)
    assert n_edge % 2 == 0
    rb = rel_bias.astype(F32)
    h = rb.shape[0]
    f = jnp.concatenate([jnp.broadcast_to(rb[:, -1:], (h, n_edge // 2)), rb[:, ::-1],
                         jnp.broadcast_to(rb[:, :1], (h, n_edge // 2 + 1))], axis=1)
    f = jnp.roll(f, -(QB - 1), axis=1)
    toep = jnp.tile(f, (1, QB))[:, :QB * (length - 1)].reshape(h, QB, length - 1)[:, :, :width]

    cpb = QB // CHUNK
    r = np.arange(QB)
    c = np.arange(A_BAND)
    tiles = []
    for v in range(A_VARIANTS):
        c0 = v * cpb
        band_start = max(c0 - N_LEFT_CHUNKS, 0)
        delta = (c0 + r[:, None] // CHUNK) - (band_start + c[None, :] // CHUNK)
        ok = (delta >= 0) & (delta <= N_LEFT_CHUNKS)
        col0 = pad - (c0 - band_start) * CHUNK
        tiles.append(jnp.where(ok[None], toep[:, :, col0:col0 + A_BAND], NEG))
    return jnp.stack(tiles, axis=0)


def _ordered_to_float(u):
    bits = jnp.where(u < 0, u ^ jnp.int32(-2147483648), ~u)
    return pltpu.bitcast(bits, F32)


def _for_steps(n, body):
    def main(i, carry):
        body(i * 8, 8)
        return carry
    lax.fori_loop(0, n // 8, main, 0)

    @pl.when((n & 4) != 0)
    def _quad():
        body((n // 8) * 8, 4)
    done = (n // 4) * 4

    @pl.when((n & 2) != 0)
    def _pair():
        body(done, 2)

    @pl.when((n & 1) != 0)
    def _single():
        body(n - 1, 1)


def _bit_transpose(words):
    a = list(words)
    j, mask = KEY_BITS // 2, 0x0000FFFF
    while j:
        k = 0
        while k < KEY_BITS:
            t = (a[k] ^ lax.shift_right_logical(a[k + j], jnp.int32(j))) & mask
            a[k] = a[k] ^ t
            a[k + j] = a[k + j] ^ jnp.left_shift(t, jnp.int32(j))
            k = (k + j + 1) & ~j
        j >>= 1
        mask ^= (mask << j) & 0xFFFFFFFF
    return a


def _attn_b_kernel(q_ref, iq_ref, iw_ref, kb_ref, vb_ref, ik_ref, o_ref,
                   st_ref, plane_ref, sel_ref, s_ref, mrun_ref, acc_ref):
    qi = pl.program_id(1)
    n_tiles = qi + 1
    n_steps = (n_tiles + 1) // 2
    n_max = st_ref.shape[0] // KT
    rows = B_HEADS * QB
    int_min = jnp.int32(-2 ** 31)

    @pl.when(qi == 0)
    def _init_planes():
        plane_ref[...] = jnp.zeros(plane_ref.shape, jnp.int32)

    qs = q_ref[0].reshape(rows, LANES)
    iqs = iq_ref[0].reshape(rows, LANES)
    iw_t = iw_ref[0].T

    key_iota = lax.broadcasted_iota(jnp.int32, (KT, LANES), 0)
    q_lane = _lane_iota((1, LANES))
    key_limit = qi * QB + CHUNK + jnp.where(q_lane >= CHUNK, CHUNK, 0)

    def score_steps(t0, cnt):
        for g in range(cnt):
            off = pl.multiple_of((t0 + g) * KT, KT)
            lg = lax.dot_general(ik_ref[0, pl.ds(off, KT), :], iqs, NT_DIMS,
                                 preferred_element_type=F32)
            sc = None
            for h in range(IDX_HEADS):
                term = jnp.maximum(lg[:, h * QB:(h + 1) * QB], 0.0) * iw_t[h:h + 1, :]
                sc = term if sc is None else sc + term
            sc = jnp.where(key_iota + off < key_limit, sc + 0.0, -jnp.inf)
            st_ref[pl.ds(off, KT), :] = sc
            bits = pltpu.bitcast(sc, jnp.int32)
            key = bits ^ (jnp.right_shift(bits, 31) & 0x7FFFFFFF) ^ int_min
            planes = _bit_transpose([key[j * SUBLANES:(j + 1) * SUBLANES] for j in range(KEY_BITS)])
            row0 = pl.multiple_of((t0 + g) * SUBLANES, SUBLANES)
            for i in range(KEY_BITS):
                plane_ref[i, pl.ds(row0, SUBLANES), :] = planes[i]
    _for_steps(n_steps, score_steps)

    sel_ref[0:1, :] = jnp.full((1, LANES), -jnp.inf, F32)
    sel_ref[1:2, :] = jnp.full((1, LANES), -1.0, F32)

    def count(pred):
        def body(t, acc):
            off = pl.multiple_of(t * KT, KT)
            hit = jnp.where(pred(st_ref[pl.ds(off, KT), :], key_iota + off), 1.0, 0.0)
            return acc + hit[:QB] + hit[QB:]

        def body4(i, acc):
            for g in range(4):
                acc = body(i * 4 + g, acc)
            return acc
        acc = lax.fori_loop(0, n_steps // 4, body4, jnp.zeros((QB, LANES), F32))
        acc = lax.fori_loop((n_steps // 4) * 4, n_steps, body, acc)
        return jnp.sum(acc, axis=0, keepdims=True)

    k_sel = float(TOPK_MAX)

    @pl.when(n_tiles * QB - CHUNK > TOPK_MAX)
    def _search():
        step_rows = lax.broadcasted_iota(jnp.int32, (n_max * SUBLANES, LANES), 0)
        alive0 = jnp.where(step_rows < n_steps * SUBLANES, -1, 0)

        def bit_body(i, carry):
            alive, rem, key = carry
            ones = alive & plane_ref[i]
            n_ones = jnp.sum(lax.population_count(ones), axis=0, keepdims=True)
            take = n_ones >= rem
            alive = jnp.where(take, ones, alive ^ ones)
            rem = jnp.where(take, rem, rem - n_ones)
            key = key | jnp.where(take, jnp.left_shift(jnp.int32(1), KEY_BITS - 1 - i), 0)
            return alive, rem, key
        alive, rem, key = lax.fori_loop(0, KEY_BITS, bit_body,
                                  (alive0, jnp.full((1, LANES), TOPK_MAX, jnp.int32),
                                   jnp.zeros((1, LANES), jnp.int32)))
        key = key ^ int_min

        def publish(thr):
            n_ge = count(lambda s, k: s >= thr)
            sel_ref[0:1, :] = thr
            sel_ref[3:4, :] = n_ge
            sel_ref[2:3, :] = jnp.zeros((1, LANES), F32)

            @pl.when(jnp.max(jnp.abs(n_ge - k_sel)) > 0.0)
            def _count_above():
                sel_ref[2:3, :] = count(lambda s, k: s > thr)
        publish(pltpu.bitcast(key ^ (jnp.right_shift(key, 31) & 0x7FFFFFFF), F32))

        holds = (sel_ref[2:3, :] < k_sel) & (sel_ref[3:4, :] >= k_sel)

        @pl.when(jnp.min(jnp.where(holds, 1.0, 0.0)) < 1.0)
        def _float_search():
            def bit_body(i, u):
                trial = u | jnp.left_shift(jnp.int32(1), 31 - i)
                cand = _ordered_to_float(trial)
                return jnp.where(count(lambda s, k: s >= cand) >= k_sel, trial, u)
            publish(_ordered_to_float(lax.fori_loop(0, 32, bit_body, jnp.zeros((1, LANES), jnp.int32))))

        thr = sel_ref[0:1, :]
        n_ge = sel_ref[3:4, :]
        need = k_sel - sel_ref[2:3, :]
        sel_ref[1:2, :] = jnp.full((1, LANES), 2.0 ** 30, F32)

        @pl.when(jnp.max(n_ge) > k_sel)
        def _ties():
            word_key = (step_rows // SUBLANES) * KT + step_rows % SUBLANES
            ones = jnp.full(step_rows.shape, -1, jnp.int32)

            def plane_body(i, x):
                trial = x | jnp.left_shift(jnp.int32(1), 11 - i)
                n_low = jnp.clip(jnp.right_shift(trial - word_key + (SUBLANES - 1), 3), 0, KEY_BITS)
                rest = lax.shift_right_logical(ones, jnp.minimum(n_low, KEY_BITS - 1))
                below = jnp.where(n_low >= KEY_BITS, ones, ~rest)
                cnt = jnp.sum(lax.population_count(alive & below), axis=0, keepdims=True)
                return jnp.where(cnt < rem, trial, x)
            x = lax.fori_loop(0, 12, plane_body, jnp.zeros((1, LANES), jnp.int32))
            sel_ref[1:2, :] = x.astype(F32)

            n_cut = count(lambda s, k: (s == thr) & (k <= x))

            @pl.when(jnp.min(jnp.where(n_cut == need, 1.0, 0.0)) < 1.0)
            def _float_ties():
                def idx_body(i, x):
                    trial = x | jnp.left_shift(jnp.int32(1), 11 - i)
                    cnt = count(lambda s, k: (s == thr) & (k < trial))
                    return jnp.where(cnt < need, trial, x)
                x = lax.fori_loop(0, 12, idx_body, jnp.zeros((1, LANES), jnp.int32))
                sel_ref[1:2, :] = x.astype(F32)

    thr = sel_ref[0:1, :]
    cut = sel_ref[1:2, :].astype(jnp.int32)

    half_iota = key_iota[:QB]
    mrun_ref[...] = jnp.full((rows, LANES), NEG, F32)

    def pass1_steps(t0, cnt):
        best = None
        for g in range(cnt):
            off = pl.multiple_of((t0 + g) * KT, KT)
            s = lax.dot_general(qs, kb_ref[0, pl.ds(off, KT), :], NT_DIMS,
                                preferred_element_type=F32)
            halves = []
            for half in range(KT // QB):
                sc = st_ref[pl.ds(off + half * QB, QB), :]
                tie = jnp.where(half_iota + (off + half * QB) <= cut, 0.0, NEG)
                halves.append(jnp.where(sc > thr, 0.0, jnp.where(sc == thr, tie, NEG)).T)
            mask = jnp.concatenate(halves, axis=1)
            s = (s.reshape(B_HEADS, QB, KT) + mask[None]).reshape(rows, KT)
            s_ref[t0 + g] = s
            top = jnp.maximum(s[:, :LANES], s[:, LANES:])
            best = top if best is None else jnp.maximum(best, top)
        mrun_ref[...] = jnp.maximum(mrun_ref[...], best)
    _for_steps(n_steps, pass1_steps)

    m_row = jnp.max(mrun_ref[...], axis=1, keepdims=True)
    mrun_ref[...] = jnp.broadcast_to(m_row, (rows, LANES))
    acc_ref[...] = jnp.zeros((rows, LANES), F32)

    def pass2_steps(t0, cnt):
        m_b = mrun_ref[...]
        probs = []
        for g in range(cnt):
            s = s_ref[t0 + g]
            for c in range(KT // LANES):
                probs.append(jnp.exp(s[:, c * LANES:(c + 1) * LANES] - m_b).astype(BF16))
        off = pl.multiple_of(t0 * KT, KT)
        acc_ref[...] += jnp.dot(jnp.concatenate(probs, axis=1), vb_ref[0, pl.ds(off, cnt * KT), :],
                                preferred_element_type=F32)
    _for_steps(n_steps, pass2_steps)

    lane = _lane_iota((QB, LANES))
    low = lane < B_HEAD_DIM
    for g in range(B_WIDTH // LANES):
        even = acc_ref[(2 * g) * QB:(2 * g + 1) * QB, :]
        odd = acc_ref[(2 * g + 1) * QB:(2 * g + 2) * QB, :]
        numer = jnp.where(low, even, pltpu.roll(odd, B_HEAD_DIM, 1))
        denom = jnp.where(low, pltpu.roll(even, B_HEAD_DIM, 1), odd)
        o_ref[0, :, g * LANES:(g + 1) * LANES] = numer / denom


def _attn_b_call(qb, iq, iw, kb, vb, ik, batch, seq):
    n_q = seq // QB
    rows = B_HEADS * QB
    blk = lambda b, i: (b * n_q + i, 0, 0, 0)
    res = lambda b, i: (b, 0, 0)
    return pl.pallas_call(
        _attn_b_kernel,
        grid=(batch, n_q),
        in_specs=[
            pl.BlockSpec((1, B_HEADS, QB, LANES), blk),
            pl.BlockSpec((1, IDX_HEADS, QB, LANES), blk),
            pl.BlockSpec((1, QB, LANES), lambda b, i: (b, i, 0)),
            pl.BlockSpec((1, seq, LANES), res),
            pl.BlockSpec((1, seq, LANES), res),
            pl.BlockSpec((1, seq, LANES), res),
        ],
        out_specs=pl.BlockSpec((1, QB, B_WIDTH), lambda b, i: (b, i, 0)),
        out_shape=jax.ShapeDtypeStruct((batch, seq, B_WIDTH), F32),
        scratch_shapes=[
            pltpu.VMEM((seq, LANES), F32),
            pltpu.VMEM((KEY_BITS, seq // KT * SUBLANES, LANES), jnp.int32),
            pltpu.VMEM((SUBLANES, LANES), F32),
            pltpu.VMEM((seq // KT, rows, KT), F32),
            pltpu.VMEM((rows, LANES), F32),
            pltpu.VMEM((rows, LANES), F32),
        ],
        compiler_params=pltpu.CompilerParams(
            dimension_semantics=("arbitrary", "arbitrary"), vmem_limit_bytes=VMEM_LIMIT),
        name="attn_b",
    )(qb, iq, iw, kb, vb, ik)


def _sigmoid(z):
    return 1.0 / (1.0 + jnp.exp(-z))


def _out_kernel(x_ref, g_ref, wg_ref, bm_ref, ya_ref, yb_ref, wa_ref, wb_ref, wo_ref, fg_ref, o_ref):
    x = x_ref[...]
    xn = _rms(x, g_ref[...]).astype(BF16)

    def branch(y_ref, col, w_ref, bias_row):
        gate = jnp.dot(xn, wg_ref[:, col:col + A_WIDTH], preferred_element_type=F32)
        y = (y_ref[...] * (gate * _sigmoid(gate))).astype(BF16)
        pr = jnp.dot(y, w_ref[...], preferred_element_type=F32)
        zcol = 2 * A_WIDTH + bias_row * D_MODEL
        z = jnp.dot(xn, wg_ref[:, zcol:zcol + D_MODEL], preferred_element_type=F32)
        return _sigmoid(z + bm_ref[bias_row:bias_row + 1, :]) * pr

    merged = branch(ya_ref, 0, wa_ref, 0) + branch(yb_ref, A_WIDTH, wb_ref, 1)
    h = x + jnp.dot(merged.astype(BF16), wo_ref[...], preferred_element_type=F32)
    o_ref[...] = _rms(h, fg_ref[...])


def _out_call(x2, gain, w2, bm, ya, yb, wa, wb, wo, fgain):
    m = x2.shape[0]
    tm = OUT_TM
    row = lambda i: (i, 0)
    full = lambda a: pl.BlockSpec(a.shape, lambda i: (0, 0), pipeline_mode=pl.Buffered(1))
    return pl.pallas_call(
        _out_kernel,
        grid=(m // tm,),
        in_specs=[
            pl.BlockSpec((tm, D_MODEL), row), full(gain), full(w2), full(bm),
            pl.BlockSpec((tm, A_WIDTH), row), pl.BlockSpec((tm, B_WIDTH), row),
            full(wa), full(wb), full(wo), full(fgain),
        ],
        out_specs=pl.BlockSpec((tm, D_MODEL), row),
        out_shape=jax.ShapeDtypeStruct((m, D_MODEL), F32),
        compiler_params=pltpu.CompilerParams(
            dimension_semantics=("arbitrary",), vmem_limit_bytes=OUT_VMEM_LIMIT),
        name="out",
    )(x2, gain, w2, bm, ya, yb, wa, wb, wo, fgain)


def _rope_tables(seq):
    pos = jnp.arange(seq, dtype=F32)[:, None]

    def pattern(head_dim, rot_dim):
        half = rot_dim // 2
        inv = ROPE_THETA ** (-jnp.arange(half, dtype=F32) / half)
        ang = pos * inv[None, :]
        cos, sin = jnp.cos(ang), jnp.sin(ang)
        ones = jnp.ones((seq, head_dim - rot_dim), F32)
        c = jnp.concatenate([cos, cos, ones], axis=1)
        s = jnp.concatenate([-sin, sin, 0.0 * ones], axis=1)
        reps = LANES // head_dim
        return jnp.tile(c, (1, reps)), jnp.tile(s, (1, reps))

    cq, sq = pattern(B_HEAD_DIM, ROT_DIM_B)
    ci, si = pattern(IDX_DIM, ROT_DIM_IDX)
    return jnp.concatenate([cq, sq, ci, si], axis=1)


def kernel(x, norm_gain, w_in, b_merge, rel_bias, w_branch_a, w_branch_b, w_out, final_norm_gain):
    batch, seq, d = x.shape
    assert d == D_MODEL and seq % PROJ_TM == 0 and norm_gain.shape[0] == 1
    m = batch * seq
    x2 = x.reshape(m, d)
    offsets = [int(o) for o in np.cumsum(SPLIT_SIZES)[:-1]]
    (w_qa, w_ka, w_va, w_ga, w_qb, w_kb, w_vb, w_gb,
     w_iq, w_ik, w_iw, w_za, w_zb) = jnp.split(w_in[0], offsets, axis=1)
    pad = jnp.zeros((d, LANES - IDX_DIM - IDX_HEADS), w_in.dtype)
    w1 = jnp.concatenate([w_qa, w_ka, w_va, w_qb, w_iq, w_kb, w_vb, w_ik, w_iw, pad], axis=1).astype(BF16)
    w2 = jnp.concatenate([w_ga, w_gb, w_za, w_zb], axis=1).astype(BF16)

    qa, ka, va, qb, iq, kb, vb, ik, iw = _proj_call(x2, norm_gain, w1, _rope_tables(seq), batch, seq)

    to3 = lambda a: a.reshape(batch, seq, a.shape[-1])
    ya = _attn_a_call(qa, to3(ka), to3(va), _rel_bias_tiles(rel_bias[0]))
    yb = _attn_b_call(qb, iq, to3(iw), to3(kb), to3(vb), to3(ik), batch, seq)

    out = _out_call(x2, norm_gain, w2, b_merge[0], ya.reshape(m, A_WIDTH), yb.reshape(m, B_WIDTH),
                    w_branch_a[0].astype(BF16), w_branch_b[0].astype(BF16), w_out[0].astype(BF16),
                    final_norm_gain.reshape(1, d))
    return out.reshape(batch, seq, d)
```
